```python
import math
import jax, jax.numpy as jnp
from jax import lax
import numpy as np

D_MODEL = 1024
BATCH = 8
SEQ = 4096
DEPTH = 4
DEC_BATCH = 32
DEC_SEQ = 16
PAST_LEN = 4096

CHUNK = 64
N_A_LAYERS = DEPTH // 2
N_B_LAYERS = DEPTH - N_A_LAYERS
N_DENSE = (DEPTH + 1) // 2
N_MOE = DEPTH // 2
N_HEADS = 16
HEAD_DIM = 64
CONV_W = 31
D_FF = 2816
N_EXPERTS = 8
TOP_K = 2
D_EXPERT = 3584
Q_BLOCK = 128
ALPHA = (2 * DEPTH) ** 0.25
BETA = (8 * DEPTH) ** -0.25
LN_EPS = 1e-5

kernel_name = "yoco_conformer_stickbreaking_moe_step"


def _layer_norm(x, g, b):
    xf = x.astype(jnp.float32)
    mu = jnp.mean(xf, axis=-1, keepdims=True)
    var = jnp.mean(jnp.square(xf - mu), axis=-1, keepdims=True)
    return ((xf - mu) * lax.rsqrt(var + LN_EPS)).astype(x.dtype) * g + b


def _modulate(x, shift, scale):
    return x * (1.0 + scale[:, None, :]) + shift[:, None, :]


def _conv_module(h, hist, w_in, b_in, w_dw, b_dw, g, b, w_out, b_out):
    a = h @ w_in + b_in
    u = a[..., :D_MODEL] * jax.nn.sigmoid(a[..., D_MODEL:])
    u_ext = jnp.concatenate([hist, u], axis=1)
    z = lax.conv_general_dilated(
        u_ext, w_dw[:, None, :], window_strides=(1,), padding="VALID",
        dimension_numbers=("NWC", "WIO", "NWC"), feature_group_count=D_MODEL) + b_dw
    z = jax.nn.silu(_layer_norm(z, g, b))
    y = z @ w_out + b_out
    return y, u_ext[:, -(CONV_W - 1):]


def _sb_block(q, k, v, q_pos, k_pos):
    z = jnp.einsum("bqhd,bkhd->bhqk", q, k,
                   preferred_element_type=jnp.float32) * (HEAD_DIM ** -0.5)
    valid = k_pos[None, :] < q_pos[:, None]
    log_keep = jnp.where(valid, jax.nn.log_sigmoid(-z), 0.0)
    tail = lax.cumsum(log_keep, axis=3, reverse=True) - log_keep
    weights = jnp.where(valid, jnp.exp(jax.nn.log_sigmoid(z) + tail), 0.0)
    return jnp.einsum("bhqk,bkhd->bqhd", weights.astype(v.dtype), v)


def _stick_breaking(q, k, v, past_len):
    bsz, t, nh, hd = q.shape
    k_pos = jnp.arange(k.shape[1])
    q_pos = past_len + jnp.arange(t)
    if t <= Q_BLOCK:
        return _sb_block(q, k, v, q_pos, k_pos)
    nb = t // Q_BLOCK
    q_blocks = q.reshape(bsz, nb, Q_BLOCK, nh, hd).transpose(1, 0, 2, 3, 4)
    pos_blocks = q_pos.reshape(nb, Q_BLOCK)
    out = lax.map(lambda a: _sb_block(a[0], k, v, a[1], k_pos), (q_blocks, pos_blocks))
    return out.transpose(1, 0, 2, 3, 4).reshape(bsz, t, nh, hd)


def _swiglu(h, w1, w3, w2):
    return (jax.nn.silu(h @ w1) * (h @ w3)) @ w2


def _moe(h, router, router_b, w1, w3, w2):
    logits = jnp.einsum("btd,de->bte", h, router,
                        preferred_element_type=jnp.float32) + router_b.astype(jnp.float32)
    top_val, top_idx = lax.top_k(logits, TOP_K)
    gates = jax.nn.softmax(top_val, axis=-1)
    combine = jnp.sum(jax.nn.one_hot(top_idx, N_EXPERTS, dtype=jnp.float32)
                      * gates[..., None], axis=-2).astype(h.dtype)
    out = jnp.zeros_like(h)
    for e in range(N_EXPERTS):
        out = out + combine[..., e:e + 1] * _swiglu(h, w1[e], w3[e], w2[e])
    return out


def setup_inputs(seed: int = 0) -> dict:
    key = jax.random.key(seed)
    ks = jax.random.split(key, 40)
    D = D_MODEL
    HD = N_HEADS * HEAD_DIM

    def nrm(k, shape, s):
        return jax.random.normal(k, shape, jnp.float32) * s

    return {
        "x_prompt": nrm(ks[0], (BATCH, SEQ, D), 1.0),
        "x_sample": nrm(ks[1], (DEC_BATCH, DEC_SEQ, D), 1.0),
        "cache_k": nrm(ks[2], (DEC_BATCH, PAST_LEN, N_HEADS, HEAD_DIM), 1.0),
        "cache_v": nrm(ks[3], (DEC_BATCH, PAST_LEN, N_HEADS, HEAD_DIM), 1.0),
        "state_conv": nrm(ks[4], (N_A_LAYERS, DEC_BATCH, CONV_W - 1, D), 0.5),
        "c_prompt": nrm(ks[5], (BATCH, D), 1.0),
        "c_sample": nrm(ks[6], (DEC_BATCH, D), 1.0),
        "w_ada": nrm(ks[7], (DEPTH, D, 6 * D), 0.5 * D ** -0.5),
        "b_ada": nrm(ks[8], (DEPTH, 6 * D), 0.02),
        "ln_g": 1.0 + nrm(ks[9], (DEPTH, 2, D), 0.02),
        "ln_b": nrm(ks[10], (DEPTH, 2, D), 0.02),
        "conv_w_in": nrm(ks[11], (N_A_LAYERS, D, 2 * D), D ** -0.5),
        "conv_b_in": nrm(ks[12], (N_A_LAYERS, 2 * D), 0.02),
        "conv_w_dw": nrm(ks[13], (N_A_LAYERS, CONV_W, D), CONV_W ** -0.5),
        "conv_b_dw": nrm(ks[14], (N_A_LAYERS, D), 0.02),
        "conv_ln_g": 1.0 + nrm(ks[15], (N_A_LAYERS, D), 0.02),
        "conv_ln_b": nrm(ks[16], (N_A_LAYERS, D), 0.02),
        "conv_w_out": nrm(ks[17], (N_A_LAYERS, D, D), BETA * D ** -0.5),
        "conv_b_out": nrm(ks[18], (N_A_LAYERS, D), 0.02),
        "w_kv": nrm(ks[19], (D, 2 * HD), D ** -0.5),
        "w_q": nrm(ks[20], (N_B_LAYERS, D, HD), D ** -0.5),
        "w_o": nrm(ks[21], (N_B_LAYERS, HD, D), BETA * HD ** -0.5),
        "ffn_w1": nrm(ks[22], (N_DENSE, D, D_FF), D ** -0.5),
        "ffn_w3": nrm(ks[23], (N_DENSE, D, D_FF), D ** -0.5),
        "ffn_w2": nrm(ks[24], (N_DENSE, D_FF, D), BETA * D_FF ** -0.5),
        "moe_router": nrm(ks[25], (N_MOE, D, N_EXPERTS), D ** -0.5),
        "moe_router_b": nrm(ks[26], (N_MOE, N_EXPERTS), 0.01),
        "moe_w1": nrm(ks[27], (N_MOE, N_EXPERTS, D, D_EXPERT), D ** -0.5),
        "moe_w3": nrm(ks[28], (N_MOE, N_EXPERTS, D, D_EXPERT), D ** -0.5),
        "moe_w2": nrm(ks[29], (N_MOE, N_EXPERTS, D_EXPERT, D), BETA * D_EXPERT ** -0.5),
    }


def reference(x_prompt, x_sample, cache_k, cache_v, state_conv, c_prompt, c_sample,
              w_ada, b_ada, ln_g, ln_b,
              conv_w_in, conv_b_in, conv_w_dw, conv_b_dw, conv_ln_g, conv_ln_b,
              conv_w_out, conv_b_out,
              w_kv, w_q, w_o,
              ffn_w1, ffn_w3, ffn_w2,
              moe_router, moe_router_b, moe_w1, moe_w3, moe_w2):

    def trunk(x, c, conv_hist, past_k, past_v):
        bsz, t, _ = x.shape
        past_len = past_k.shape[1]
        c_act = jax.nn.silu(c)
        new_hist = []
        k_new = v_new = k_all = v_all = None
        for l in range(DEPTH):
            mod = c_act @ w_ada[l] + b_ada[l]
            sh1, sc1, g1, sh2, sc2, g2 = jnp.split(mod, 6, axis=-1)
            h = _modulate(x, sh1, sc1)
            if l < N_A_LAYERS:
                i = l
                y, hist_i = _conv_module(h, conv_hist[i], conv_w_in[i], conv_b_in[i],
                                         conv_w_dw[i], conv_b_dw[i], conv_ln_g[i],
                                         conv_ln_b[i], conv_w_out[i], conv_b_out[i])
                new_hist.append(hist_i)
            else:
                j = l - N_A_LAYERS
                q = (h @ w_q[j]).reshape(bsz, t, N_HEADS, HEAD_DIM)
                o = _stick_breaking(q, k_all, v_all, past_len)
                y = o.reshape(bsz, t, N_HEADS * HEAD_DIM) @ w_o[j]
            x = _layer_norm(ALPHA * x + (1.0 + g1)[:, None, :] * y, ln_g[l, 0], ln_b[l, 0])
            h = _modulate(x, sh2, sc2)
            if l % 2 == 0:
                m = l // 2
                y = _swiglu(h, ffn_w1[m], ffn_w3[m], ffn_w2[m])
            else:
                m = l // 2
                y = _moe(h, moe_router[m], moe_router_b[m], moe_w1[m], moe_w3[m], moe_w2[m])
            x = _layer_norm(ALPHA * x + (1.0 + g2)[:, None, :] * y, ln_g[l, 1], ln_b[l, 1])
            if l == N_A_LAYERS - 1:
                kv = x @ w_kv
                k_new = kv[..., :N_HEADS * HEAD_DIM].reshape(bsz, t, N_HEADS, HEAD_DIM)
                v_new = kv[..., N_HEADS * HEAD_DIM:].reshape(bsz, t, N_HEADS, HEAD_DIM)
                k_all = jnp.concatenate([past_k, k_new], axis=1)
                v_all = jnp.concatenate([past_v, v_new], axis=1)
        return x, k_new, v_new, jnp.stack(new_hist, axis=0)

    zero_hist = jnp.zeros((N_A_LAYERS, x_prompt.shape[0], CONV_W - 1, D_MODEL), x_prompt.dtype)
    no_past = jnp.zeros((x_prompt.shape[0], 0, N_HEADS, HEAD_DIM), x_prompt.dtype)
    y_prompt, k_prompt, v_prompt, conv_prompt = trunk(x_prompt, c_prompt, zero_hist,
                                                      no_past, no_past)
    y_sample, k_sample, v_sample, conv_sample = trunk(x_sample, c_sample, state_conv,
                                                      cache_k, cache_v)
    return (y_prompt, y_sample, k_prompt, v_prompt, conv_prompt, k_sample, v_sample, conv_sample)
```

```python
import functools

import jax
import jax.numpy as jnp
from jax import lax
from jax.experimental import pallas as pl
from jax.experimental.pallas import tpu as pltpu

LN_EPS = 1e-5
TOP_K = 2
LANES = 128
SUBLANES = 8
VMEM_LIMIT_BYTES = 56 * 1024 * 1024
TAIL_CUTOFF = -110.0

F32 = jnp.float32
BF16 = jnp.bfloat16


def _cparams(n_axes):
    return pltpu.CompilerParams(dimension_semantics=("arbitrary",) * n_axes,
                                vmem_limit_bytes=VMEM_LIMIT_BYTES)


def _dot(a, b):
    return jnp.dot(a, b, preferred_element_type=F32)


def _layer_norm(v, g, b):
    mu = jnp.mean(v, axis=-1, keepdims=True)
    c = v - mu
    var = jnp.mean(c * c, axis=-1, keepdims=True)
    return c * lax.rsqrt(var + LN_EPS) * g + b


def _silu(v):
    return v * jax.nn.sigmoid(v)


def _pick_tile(n, target):
    t = min(n, target)
    while n % t:
        t -= 1
    return t


def _ada_kernel(c_ref, w_ref, b_ref, o_ref):
    ca = _silu(c_ref[...]).astype(BF16)
    o_ref[...] = _dot(ca, w_ref[...].astype(BF16)) + b_ref[...]


def _ada_table(c_all, w_ada, b_ada):
    n_layers, d, d6 = w_ada.shape
    bc = c_all.shape[0]
    tn = _pick_tile(d6, 1536)
    return pl.pallas_call(
        _ada_kernel,
        grid=(n_layers, d6 // tn),
        in_specs=[pl.BlockSpec((bc, d), lambda l, j: (0, 0)),
                  pl.BlockSpec((None, d, tn), lambda l, j: (l, 0, j)),
                  pl.BlockSpec((None, 1, tn), lambda l, j: (l, 0, j))],
        out_specs=pl.BlockSpec((None, bc, tn), lambda l, j: (l, 0, j)),
        out_shape=jax.ShapeDtypeStruct((n_layers, bc, d6), F32),
        compiler_params=_cparams(2),
        name="ada_table",
    )(c_all, w_ada, b_ada.reshape(n_layers, 1, d6))


class _Group:
    def __init__(self, b, t, row_off, rows_target):
        self.b, self.t = b, t
        self.tt = _pick_tile(t, rows_target)
        self.bb = _pick_tile(b, max(1, rows_target // self.tt))
        assert row_off % self.bb == 0
        self.boff = row_off // self.bb
        self.grid = (b // self.bb, t // self.tt)
        self.tm = self.bb * self.tt

    def x_spec(self, d):
        return pl.BlockSpec((self.bb, self.tt, d), lambda b, i: (b, i, 0))

    def mod_spec(self, d, layer, which):
        boff = self.boff
        return pl.BlockSpec((None, self.bb, None, 1, d),
                            lambda b, i: (layer, boff + b, which, 0, 0))


def _const_spec(shape):
    nd = len(shape)
    return pl.BlockSpec(shape, lambda b, i: (0,) * nd)


def _conv_in_kernel(x_ref, sh_ref, sc_ref, w_ref, b_ref, u_ref):
    bb, tt, d = x_ref.shape
    h = x_ref[...] * (1.0 + sc_ref[...]) + sh_ref[...]
    hb = h.reshape(bb * tt, d).astype(BF16)
    a = _dot(hb, w_ref[:, :d]) + b_ref[:, :d]
    g = _dot(hb, w_ref[:, d:]) + b_ref[:, d:]
    u_ref[...] = (a * jax.nn.sigmoid(g)).reshape(bb, tt, d)


def _conv_in(grp, x, mod5, layer, w_in, b_in):
    d = x.shape[-1]
    return pl.pallas_call(
        _conv_in_kernel,
        grid=grp.grid,
        in_specs=[grp.x_spec(d), grp.mod_spec(d, layer, 0), grp.mod_spec(d, layer, 1),
                  _const_spec(w_in.shape), _const_spec((1, 2 * d))],
        out_specs=grp.x_spec(d),
        out_shape=jax.ShapeDtypeStruct(x.shape, F32),
        compiler_params=_cparams(2),
        name="conv_in",
    )(x, mod5, mod5, w_in, b_in.reshape(1, 2 * d))


CONV_CHUNK_ROWS = 32
CONV_CHUNK_LANES = 512


def _conv_out_kernel(*refs, alpha, kw, use_halo):
    if use_halo:
        (u_ref, halo_ref, hist_ref, x_ref, g_ref, wdw_ref, bdw_ref, cg_ref, cb_ref,
         w_ref, b_ref, lg_ref, lb_ref, o_ref, ext_ref, z_ref) = refs
    else:
        (u_ref, hist_ref, x_ref, g_ref, wdw_ref, bdw_ref, cg_ref, cb_ref,
         w_ref, b_ref, lg_ref, lb_ref, o_ref, ext_ref, z_ref) = refs
    bb, tt, d = u_ref.shape
    hp = ext_ref.shape[1] - tt
    ext_ref[:, hp:, :] = u_ref[...]
    if use_halo:
        first = pl.program_id(1) == 0

        @pl.when(first)
        def _():
            ext_ref[:, :hp, :] = hist_ref[...]

        @pl.when(jnp.logical_not(first))
        def _():
            ext_ref[:, :hp, :] = halo_ref[...]
    else:
        ext_ref[:, :hp, :] = hist_ref[...]

    off = hp - (kw - 1)
    rc = min(tt, CONV_CHUNK_ROWS)
    bc = min(bb, CONV_CHUNK_ROWS // rc)
    cc = min(d, CONV_CHUNK_LANES)
    for b0 in range(0, bb, bc):
        for r0 in range(0, tt, rc):
            for c0 in range(0, d, cc):
                acc = jnp.broadcast_to(bdw_ref[:, c0:c0 + cc], (bc, rc, cc))
                for k in range(kw):
                    s0 = off + r0 + k
                    acc = acc + wdw_ref[k, :, c0:c0 + cc] * ext_ref[b0:b0 + bc, s0:s0 + rc, c0:c0 + cc]
                z_ref[b0:b0 + bc, r0:r0 + rc, c0:c0 + cc] = acc

    z = z_ref[...].reshape(bb * tt, d)
    za = _silu(_layer_norm(z, cg_ref[...], cb_ref[...])).astype(BF16)
    y = (_dot(za, w_ref[...]) + b_ref[...]).reshape(bb, tt, d)
    v = alpha * x_ref[...] + (1.0 + g_ref[...]) * y
    o_ref[...] = _layer_norm(v, lg_ref[...], lb_ref[...])


def _conv_out(grp, u, hist_pad, x, mod5, layer, w_dw8, b_dw, cg, cb, w_out, b_out, lg, lb, alpha):
    d = x.shape[-1]
    kw = w_dw8.shape[0]
    hp = hist_pad.shape[1]
    use_halo = grp.grid[1] > 1
    tt = grp.tt
    row = lambda a: a.reshape(1, d)
    in_specs = [grp.x_spec(d)]
    args = [u]
    if use_halo:
        assert grp.bb == 1 and tt % hp == 0
        in_specs.append(pl.BlockSpec((1, hp, d), lambda b, i: (b, jnp.maximum(i * (tt // hp) - 1, 0), 0)))
        args.append(u)
    in_specs += [pl.BlockSpec((grp.bb, hp, d), lambda b, i: (b, 0, 0)),
                 grp.x_spec(d), grp.mod_spec(d, layer, 2),
                 _const_spec(w_dw8.shape), _const_spec((1, d)), _const_spec((1, d)), _const_spec((1, d)),
                 _const_spec(w_out.shape), _const_spec((1, d)), _const_spec((1, d)), _const_spec((1, d))]
    args += [hist_pad, x, mod5, w_dw8, row(b_dw), row(cg), row(cb), w_out, row(b_out), row(lg), row(lb)]
    return pl.pallas_call(
        functools.partial(_conv_out_kernel, alpha=alpha, kw=kw, use_halo=use_halo),
        grid=grp.grid,
        in_specs=in_specs,
        out_specs=grp.x_spec(d),
        out_shape=jax.ShapeDtypeStruct(x.shape, F32),
        scratch_shapes=[pltpu.VMEM((grp.bb, hp + tt, d), F32), pltpu.VMEM((grp.bb, tt, d), F32)],
        compiler_params=_cparams(2),
        name="conv_out",
    )(*args)


def _ffn_kernel(x_ref, sh_ref, sc_ref, g_ref, w1_ref, w3_ref, w2_ref, lg_ref, lb_ref, o_ref, *, alpha, fc):
    bb, tt, d = x_ref.shape
    f = w1_ref.shape[1]
    x = x_ref[...]
    hb = (x * (1.0 + sc_ref[...]) + sh_ref[...]).reshape(bb * tt, d).astype(BF16)
    y = jnp.zeros((bb * tt, d), F32)
    for f0 in range(0, f, fc):
        a = _dot(hb, w1_ref[:, f0:f0 + fc])
        b = _dot(hb, w3_ref[:, f0:f0 + fc])
        y = y + _dot((_silu(a) * b).astype(BF16), w2_ref[f0:f0 + fc, :])
    v = alpha * x + (1.0 + g_ref[...]) * y.reshape(bb, tt, d)
    o_ref[...] = _layer_norm(v, lg_ref[...], lb_ref[...])


def _ffn(grp, x, mod5, layer, w1, w3, w2, lg, lb, alpha):
    d = x.shape[-1]
    f = w1.shape[1]
    fc = f // 2 if (f // 2) % LANES == 0 else f
    return pl.pallas_call(
        functools.partial(_ffn_kernel, alpha=alpha, fc=fc),
        grid=grp.grid,
        in_specs=[grp.x_spec(d), grp.mod_spec(d, layer, 3), grp.mod_spec(d, layer, 4), grp.mod_spec(d, layer, 5),
                  _const_spec(w1.shape), _const_spec(w3.shape), _const_spec(w2.shape),
                  _const_spec((1, d)), _const_spec((1, d))],
        out_specs=grp.x_spec(d),
        out_shape=jax.ShapeDtypeStruct(x.shape, F32),
        compiler_params=_cparams(2),
        name="ffn",
    )(x, mod5, mod5, mod5, w1, w3, w2, lg.reshape(1, d), lb.reshape(1, d))


def _router_kernel(x_ref, sh_ref, sc_ref, rw_ref, rb_ref, h_ref, idx_ref, gate_ref, *, n_experts):
    bb, tt, d = x_ref.shape
    h = (x_ref[...] * (1.0 + sc_ref[...]) + sh_ref[...]).reshape(bb * tt, d)
    h_ref[...] = h
    rw = rw_ref[...]
    h_hi = h.astype(BF16)
    h_lo = (h - h_hi.astype(F32)).astype(BF16)
    r_hi = rw.astype(BF16)
    r_lo = (rw - r_hi.astype(F32)).astype(BF16)
    logits = _dot(h_hi, r_hi) + (_dot(h_lo, r_hi) + _dot(h_hi, r_lo)) + rb_ref[...]
    lane = lax.broadcasted_iota(jnp.int32, logits.shape, 1)
    neg = jnp.float32(-jnp.inf)
    lg = jnp.where(lane < n_experts, logits, neg)
    m1 = jnp.max(lg, axis=1, keepdims=True)
    i1 = jnp.min(jnp.where(lg == m1, lane, LANES), axis=1, keepdims=True)
    lg2 = jnp.where(lane == i1, neg, lg)
    m2 = jnp.max(lg2, axis=1, keepdims=True)
    i2 = jnp.min(jnp.where(lg2 == m2, lane, LANES), axis=1, keepdims=True)
    e2 = jnp.exp(m2 - m1)
    den = 1.0 + e2
    idx_ref[...] = jnp.where(lane == 0, i1, jnp.where(lane == 1, i2, 0))
    gate_ref[...] = jnp.where(lane == 0, 1.0 / den, jnp.where(lane == 1, e2 / den, 0.0))


def _router(grp, x, mod5, layer, rw_pad, rb_pad, n_experts):
    d = x.shape[-1]
    n = grp.b * grp.t
    nt = grp.grid[1]
    row_spec = lambda w: pl.BlockSpec((grp.tm, w), lambda b, i: (b * nt + i, 0))
    return pl.pallas_call(
        functools.partial(_router_kernel, n_experts=n_experts),
        grid=grp.grid,
        in_specs=[grp.x_spec(d), grp.mod_spec(d, layer, 3), grp.mod_spec(d, layer, 4),
                  _const_spec(rw_pad.shape), _const_spec((1, LANES))],
        out_specs=[row_spec(d), row_spec(LANES), row_spec(LANES)],
        out_shape=[jax.ShapeDtypeStruct((n, d), F32),
                   jax.ShapeDtypeStruct((n, LANES), jnp.int32),
                   jax.ShapeDtypeStruct((n, LANES), F32)],
        compiler_params=_cparams(2),
        name="router",
    )(x, mod5, mod5, rw_pad, rb_pad)


MOE_ROW_TILE = 1024
MOE_F_TILE = 512
DMA_UNROLL = 8


def _moe_kernel(texp_ref, tval_ref, src_cur_ref, src_nxt_ref, dst_ref, h_hbm,
                w1_ref, w3_ref, w2_ref, out_hbm, xbuf, xb16, acc, stage, gsem, ssem):
    i = pl.program_id(0)
    j = pl.program_id(1)
    nt = pl.num_programs(0)
    nj = pl.num_programs(1)
    tm = xb16.shape[0]
    slot = i % 2

    def gather_copy(row, r, s):
        return pltpu.make_async_copy(h_hbm.at[pl.ds(row, 1)], xbuf.at[s, pl.ds(r, 1)], gsem.at[s])

    def scatter_copy(row, r, s):
        return pltpu.make_async_copy(stage.at[s, pl.ds(r, 1)], out_hbm.at[pl.ds(row, 1)], ssem.at[s])

    def start_gather(src_ref, s):
        def body(r, c):
            gather_copy(src_ref[0, 0, r], r, s).start()
            return c
        lax.fori_loop(0, tm, body, 0, unroll=DMA_UNROLL)

    def wait_gather(s):
        pltpu.make_async_copy(h_hbm.at[pl.ds(0, tm)], xbuf.at[s], gsem.at[s]).wait()

    def wait_scatter(s):
        pltpu.make_async_copy(stage.at[s], out_hbm.at[pl.ds(0, tm)], ssem.at[s]).wait()

    @pl.when(j == 0)
    def _():
        @pl.when(i == 0)
        def _():
            start_gather(src_cur_ref, 0)

        wait_gather(slot)

        @pl.when(i + 1 < nt)
        def _():
            start_gather(src_nxt_ref, 1 - slot)

        xb16[...] = xbuf[slot].astype(BF16)
        acc[...] = jnp.zeros_like(acc)

    @pl.when(tval_ref[i] == 1)
    def _():
        xb = xb16[...]
        a = _dot(xb, w1_ref[...].astype(BF16))
        b = _dot(xb, w3_ref[...].astype(BF16))
        acc[...] += _dot((_silu(a) * b).astype(BF16), w2_ref[...].astype(BF16))

    @pl.when(j == nj - 1)
    def _():
        @pl.when(i >= 2)
        def _():
            wait_scatter(slot)

        stage[slot] = acc[...]

        def body(r, c):
            scatter_copy(dst_ref[0, 0, r], r, slot).start()
            return c
        lax.fori_loop(0, tm, body, 0, unroll=DMA_UNROLL)

        @pl.when(i == nt - 1)
        def _():
            wait_scatter(slot)

            @pl.when(i >= 1)
            def _():
                wait_scatter(1 - slot)


def _moe_plan(idx, n_experts, tm):
    n = idx.shape[0]
    m = TOP_K * n
    n_tiles = m // tm + n_experts
    e_flat = idx.T.reshape(m)
    order = jnp.argsort(e_flat, stable=True).astype(jnp.int32)
    counts = jnp.sum(e_flat[:, None] == jnp.arange(n_experts)[None, :], axis=0).astype(jnp.int32)
    padded = ((counts + tm - 1) // tm) * tm
    pend = jnp.cumsum(padded)
    pstart = pend - padded
    ustart = jnp.cumsum(counts) - counts
    tile_first = jnp.arange(n_tiles, dtype=jnp.int32) * tm
    tile_exp = jnp.minimum(jnp.sum(tile_first[:, None] >= pend[None, :], axis=1), n_experts - 1).astype(jnp.int32)
    tile_valid = (tile_first < pend[-1]).astype(jnp.int32)
    p = jnp.arange(n_tiles * tm, dtype=jnp.int32)
    pe = jnp.repeat(tile_exp, tm)
    within = p - pstart[pe]
    real = (within < counts[pe]) & jnp.repeat(tile_valid == 1, tm)
    slot = order[jnp.clip(ustart[pe] + within, 0, m - 1)]
    src = jnp.where(real, slot % n, 0).astype(jnp.int32)
    pad_rank = jnp.cumsum(jnp.logical_not(real).astype(jnp.int32)) - 1
    dst = jnp.where(real, slot, m + pad_rank).astype(jnp.int32)
    return (tile_exp, tile_valid, src.reshape(n_tiles, 1, tm), dst.reshape(n_tiles, 1, tm), n_tiles)


def _moe_experts(h, idx, w1, w3, w2):
    n, d = h.shape
    n_experts, _, f = w1.shape
    tm = _pick_tile(TOP_K * n, MOE_ROW_TILE)
    tf = _pick_tile(f, MOE_F_TILE)
    tile_exp, tile_valid, src, dst, n_tiles = _moe_plan(idx, n_experts, tm)
    out_rows = (TOP_K * n // tm + n_experts) * tm + 0
    smem_spec = lambda fn: pl.BlockSpec((1, 1, tm), fn, memory_space=pltpu.SMEM)
    grid_spec = pltpu.PrefetchScalarGridSpec(
        num_scalar_prefetch=2,
        grid=(n_tiles, f // tf),
        in_specs=[smem_spec(lambda i, j, te, tv: (i, 0, 0)),
                  smem_spec(lambda i, j, te, tv: (jnp.minimum(i + 1, n_tiles - 1), 0, 0)),
                  smem_spec(lambda i, j, te, tv: (i, 0, 0)),
                  pl.BlockSpec(memory_space=pl.ANY),
                  pl.BlockSpec((None, d, tf), lambda i, j, te, tv: (te[i], 0, j)),
                  pl.BlockSpec((None, d, tf), lambda i, j, te, tv: (te[i], 0, j)),
                  pl.BlockSpec((None, tf, d), lambda i, j, te, tv: (te[i], j, 0))],
        out_specs=pl.BlockSpec(memory_space=pl.ANY),
        scratch_shapes=[pltpu.VMEM((2, tm, d), F32), pltpu.VMEM((tm, d), BF16), pltpu.VMEM((tm, d), F32),
                        pltpu.VMEM((2, tm, d), F32), pltpu.SemaphoreType.DMA((2,)), pltpu.SemaphoreType.DMA((2,))],
    )
    return pl.pallas_call(
        _moe_kernel,
        grid_spec=grid_spec,
        out_shape=jax.ShapeDtypeStruct((out_rows, d), F32),
        compiler_params=_cparams(2),
        name="moe_experts",
    )(tile_exp, tile_valid, src, src, dst, h, w1, w3, w2)


def _combine_kernel(x_ref, g_ref, y0_ref, y1_ref, gate_ref, lg_ref, lb_ref, o_ref, *, alpha):
    bb, tt, d = x_ref.shape
    gt = gate_ref[...]
    y = gt[:, 0:1] * y0_ref[...] + gt[:, 1:2] * y1_ref[...]
    v = alpha * x_ref[...] + (1.0 + g_ref[...]) * y.reshape(bb, tt, d)
    o_ref[...] = _layer_norm(v, lg_ref[...], lb_ref[...])


def _moe_combine(grp, x, mod5, layer, y2, gates, lg, lb, alpha):
    d = x.shape[-1]
    n = grp.b * grp.t
    nt = grp.grid[1]
    tm = grp.tm
    k1 = n // tm
    return pl.pallas_call(
        functools.partial(_combine_kernel, alpha=alpha),
        grid=grp.grid,
        in_specs=[grp.x_spec(d), grp.mod_spec(d, layer, 5),
                  pl.BlockSpec((tm, d), lambda b, i: (b * nt + i, 0)),
                  pl.BlockSpec((tm, d), lambda b, i: (k1 + b * nt + i, 0)),
                  pl.BlockSpec((tm, LANES), lambda b, i: (b * nt + i, 0)),
                  _const_spec((1, d)), _const_spec((1, d))],
        out_specs=grp.x_spec(d),
        out_shape=jax.ShapeDtypeStruct(x.shape, F32),
        compiler_params=_cparams(2),
        name="moe_combine",
    )(x, mod5, y2, y2, gates, lg.reshape(1, d), lb.reshape(1, d))


def _kv_kernel(x_ref, w_ref, k_ref, v_ref, kb_ref, vb_ref):
    bb, tt, d = x_ref.shape
    hd = k_ref.shape[-1]
    xb = x_ref[...].reshape(bb * tt, d).astype(BF16)
    k = _dot(xb, w_ref[:, :hd]).reshape(bb, tt, hd)
    v = _dot(xb, w_ref[:, hd:]).reshape(bb, tt, hd)
    k_ref[...] = k
    v_ref[...] = v
    kb_ref[...] = k.astype(BF16)
    vb_ref[...] = v.astype(BF16)


def _kv_proj(grp, x, w_kv):
    d = x.shape[-1]
    hd = w_kv.shape[1] // 2
    shp = (grp.b, grp.t, hd)
    return pl.pallas_call(
        _kv_kernel,
        grid=grp.grid,
        in_specs=[grp.x_spec(d), _const_spec(w_kv.shape)],
        out_specs=[grp.x_spec(hd)] * 4,
        out_shape=[jax.ShapeDtypeStruct(shp, F32), jax.ShapeDtypeStruct(shp, F32),
                   jax.ShapeDtypeStruct(shp, BF16), jax.ShapeDtypeStruct(shp, BF16)],
        compiler_params=_cparams(2),
        name="kv_proj",
    )(x, w_kv)


def _q_kernel(x_ref, sh_ref, sc_ref, w_ref, q_ref, *, scale):
    bb, tt, d = x_ref.shape
    hb = (x_ref[...] * (1.0 + sc_ref[...]) + sh_ref[...]).reshape(bb * tt, d).astype(BF16)
    q_ref[...] = (_dot(hb, w_ref[...]) * scale).reshape(q_ref.shape).astype(BF16)


def _q_proj(grp, x, mod5, layer, w_q, scale):
    d = x.shape[-1]
    hd = w_q.shape[1]
    return pl.pallas_call(
        functools.partial(_q_kernel, scale=scale),
        grid=grp.grid,
        in_specs=[grp.x_spec(d), grp.mod_spec(d, layer, 0), grp.mod_spec(d, layer, 1), _const_spec(w_q.shape)],
        out_specs=grp.x_spec(hd),
        out_shape=jax.ShapeDtypeStruct((grp.b, grp.t, hd), BF16),
        compiler_params=_cparams(2),
        name="q_proj",
    )(x, mod5, mod5, w_q)


def _o_kernel(o_ref, x_ref, g_ref, w_ref, lg_ref, lb_ref, out_ref, *, alpha):
    bb, tt, d = x_ref.shape
    y = _dot(o_ref[...].reshape(bb * tt, o_ref.shape[-1]), w_ref[...]).reshape(bb, tt, d)
    v = alpha * x_ref[...] + (1.0 + g_ref[...]) * y
    out_ref[...] = _layer_norm(v, lg_ref[...], lb_ref[...])


def _o_proj(grp, o, x, mod5, layer, w_o, lg, lb, alpha):
    d = x.shape[-1]
    hd = w_o.shape[0]
    return pl.pallas_call(
        functools.partial(_o_kernel, alpha=alpha),
        grid=grp.grid,
        in_specs=[grp.x_spec(hd), grp.x_spec(d), grp.mod_spec(d, layer, 2), _const_spec(w_o.shape),
                  _const_spec((1, d)), _const_spec((1, d))],
        out_specs=grp.x_spec(d),
        out_shape=jax.ShapeDtypeStruct(x.shape, F32),
        compiler_params=_cparams(2),
        name="o_proj",
    )(o, x, mod5, w_o, lg.reshape(1, d), lb.reshape(1, d))


def _softplus(z):
    return jnp.maximum(z, 0.0) + jnp.log1p(jnp.exp(-jnp.abs(z)))


def _sb_block(qh, kb, vb, carry, strict_causal):
    bq, bk = qh.shape[0], kb.shape[0]
    z = lax.dot_general(qh, kb, (((1,), (1,)), ((), ())), preferred_element_type=F32)
    sp = _softplus(z)
    row = lax.broadcasted_iota(jnp.int32, (bq, bk), 0)
    col = lax.broadcasted_iota(jnp.int32, (bq, bk), 1)
    log_keep = -sp
    if strict_causal:
        valid = col < row
        log_keep = jnp.where(valid, log_keep, 0.0)
    jj = lax.broadcasted_iota(jnp.int32, (bk, bk), 0)
    ss = lax.broadcasted_iota(jnp.int32, (bk, bk), 1)
    after = jnp.where(jj > ss, 1.0, 0.0).astype(BF16)
    hi = log_keep.astype(BF16)
    lo = (log_keep - hi.astype(F32)).astype(BF16)
    tail = _dot(hi, after) + _dot(lo, after) + carry
    w = jnp.exp(z - sp + tail)
    if strict_causal:
        w = jnp.where(valid, w, 0.0)
    pv = _dot(w.astype(BF16), vb)
    return pv, carry + jnp.sum(log_keep, axis=1, keepdims=True)


def _attn_kernel(*refs, bq, bk_cache, has_cache, head_dim):
    if has_cache:
        q_ref, kn_ref, vn_ref, ck_ref, cv_ref, o_ref = refs
    else:
        q_ref, kn_ref, vn_ref, o_ref = refs
    i = pl.program_id(2)
    w = q_ref.shape[-1]
    nh = w // head_dim
    q = q_ref[...]
    lane = lax.broadcasted_iota(jnp.int32, (bq, w), 1)
    qs = [jnp.where((lane >= h * head_dim) & (lane < (h + 1) * head_dim), q, jnp.zeros_like(q)) for h in range(nh)]

    def step(kb, vb, carries, acc, strict_causal):
        new_c = []
        for h in range(nh):
            pv, c = _sb_block(qs[h], kb, vb, carries[h], strict_causal)
            acc = jnp.where((lane >= h * head_dim) & (lane < (h + 1) * head_dim), acc + pv, acc)
            new_c.append(c)
        return tuple(new_c), acc

    def alive(carries):
        m = carries[0]
        for c in carries[1:]:
            m = jnp.maximum(m, c)
        return jnp.max(m) > TAIL_CUTOFF

    zero_c = tuple(jnp.zeros((bq, 1), F32) for _ in range(nh))
    d0 = pl.multiple_of(i * bq, bq)
    carries, acc = step(kn_ref[pl.ds(d0, bq), :], vn_ref[pl.ds(d0, bq), :], zero_c,
                        jnp.zeros((bq, w), F32), True)

    def cond_new(st):
        return (st[0] >= 0) & alive(st[1])

    def body_new(st):
        jb, cs, ac = st
        k0 = pl.multiple_of(jb * bq, bq)
        cs, ac = step(kn_ref[pl.ds(k0, bq), :], vn_ref[pl.ds(k0, bq), :], cs, ac, False)
        return jb - 1, cs, ac

    _, carries, acc = lax.while_loop(cond_new, body_new, (i - 1, carries, acc))

    if has_cache:
        ncb = ck_ref.shape[0] // bk_cache

        def body_c(st):
            jb, cs, ac = st
            k0 = pl.multiple_of(jb * bk_cache, bk_cache)
            kb = ck_ref[pl.ds(k0, bk_cache), :].astype(BF16)
            vb = cv_ref[pl.ds(k0, bk_cache), :].astype(BF16)
            cs, ac = step(kb, vb, cs, ac, False)
            return jb - 1, cs, ac

        _, carries, acc = lax.while_loop(cond_new, body_c, (ncb - 1, carries, acc))

    o_ref[...] = acc.astype(o_ref.dtype)


def _attention(q, kn, vn, cache_k, cache_v, head_dim, bq_target=256, bk_cache=256):
    b, t, hd = q.shape
    w = LANES if hd % LANES == 0 else hd
    bq = _pick_tile(t, bq_target)
    has_cache = cache_k is not None
    in_specs = [pl.BlockSpec((None, bq, w), lambda b_, h_, i: (b_, i, h_)),
                pl.BlockSpec((None, t, w), lambda b_, h_, i: (b_, 0, h_)),
                pl.BlockSpec((None, t, w), lambda b_, h_, i: (b_, 0, h_))]
    args = [q, kn, vn]
    if has_cache:
        p = cache_k.shape[1]
        bk_cache = _pick_tile(p, bk_cache)
        in_specs += [pl.BlockSpec((None, p, w), lambda b_, h_, i: (b_, 0, h_))] * 2
        args += [cache_k, cache_v]
    return pl.pallas_call(
        functools.partial(_attn_kernel, bq=bq, bk_cache=bk_cache, has_cache=has_cache, head_dim=head_dim),
        grid=(b, hd // w, t // bq),
        in_specs=in_specs,
        out_specs=pl.BlockSpec((None, bq, w), lambda b_, h_, i: (b_, i, h_)),
        out_shape=jax.ShapeDtypeStruct((b, t, hd), BF16),
        compiler_params=_cparams(3),
        name="sb_attention",
    )(*args)


ROWS_PER_STEP = 512
CONV_CONTEXT_ROWS = 32


def kernel(x_prompt, x_sample, cache_k, cache_v, state_conv, c_prompt, c_sample, w_ada, b_ada, ln_g, ln_b, conv_w_in, conv_b_in, conv_w_dw, conv_b_dw, conv_ln_g, conv_ln_b, conv_w_out, conv_b_out, w_kv, w_q, w_o, ffn_w1, ffn_w3, ffn_w2, moe_router, moe_router_b, moe_w1, moe_w3, moe_w2):
    depth, d, _ = w_ada.shape
    n_a = conv_w_in.shape[0]
    kw = conv_w_dw.shape[1]
    n_heads, head_dim = cache_k.shape[2], cache_k.shape[3]
    hd = n_heads * head_dim
    n_experts = moe_router.shape[-1]
    alpha = float((2 * depth) ** 0.25)
    bp, tp, _ = x_prompt.shape
    bs, ts, _ = x_sample.shape
    past = cache_k.shape[1]
    hp = CONV_CONTEXT_ROWS
    assert kw - 1 <= hp and n_experts <= LANES

    c_all = jnp.concatenate([c_sample, c_prompt], axis=0)
    mod = _ada_table(c_all, w_ada, b_ada)
    mod5 = mod.reshape(depth, bs + bp, 6, 1, d)

    g_prompt = _Group(bp, tp, bs, ROWS_PER_STEP)
    g_sample = _Group(bs, ts, 0, ROWS_PER_STEP)

    cw_in, cw_out = conv_w_in.astype(BF16), conv_w_out.astype(BF16)
    wkv_b, wq_b, wo_b = w_kv.astype(BF16), w_q.astype(BF16), w_o.astype(BF16)
    f1_b, f3_b, f2_b = ffn_w1.astype(BF16), ffn_w3.astype(BF16), ffn_w2.astype(BF16)
    w_dw8 = conv_w_dw.reshape(n_a, kw, 1, d)
    rw_pad = jnp.pad(moe_router, ((0, 0), (0, 0), (0, LANES - n_experts)))
    rb_pad = jnp.pad(moe_router_b, ((0, 0), (0, LANES - n_experts))).reshape(-1, 1, LANES)
    cache_k2 = cache_k.reshape(bs, past, hd)
    cache_v2 = cache_v.reshape(bs, past, hd)

    def trunk(grp, x, hist, ck, cv):
        b, t = grp.b, grp.t
        hist_pad = jnp.pad(hist, ((0, 0), (0, 0), (hp - (kw - 1), 0), (0, 0)))
        new_hist = []
        k = v = kb = vb = None
        for l in range(depth):
            if l < n_a:
                u = _conv_in(grp, x, mod5, l, cw_in[l], conv_b_in[l])
                x = _conv_out(grp, u, hist_pad[l], x, mod5, l, w_dw8[l], conv_b_dw[l], conv_ln_g[l], conv_ln_b[l],
                              cw_out[l], conv_b_out[l], ln_g[l, 0], ln_b[l, 0], alpha)
                u_ext = jnp.concatenate([hist[l][:, max(0, kw - 1 - t):], u[:, max(0, t - (kw - 1)):]], axis=1)
                new_hist.append(u_ext[:, -(kw - 1):])
            else:
                j = l - n_a
                q = _q_proj(grp, x, mod5, l, wq_b[j], head_dim ** -0.5)
                o = _attention(q, kb, vb, ck, cv, head_dim)
                x = _o_proj(grp, o, x, mod5, l, wo_b[j], ln_g[l, 0], ln_b[l, 0], alpha)
            m = l // 2
            if l % 2 == 0:
                x = _ffn(grp, x, mod5, l, f1_b[m], f3_b[m], f2_b[m], ln_g[l, 1], ln_b[l, 1], alpha)
            else:
                h, idx, gates = _router(grp, x, mod5, l, rw_pad[m], rb_pad[m], n_experts)
                y2 = _moe_experts(h, idx[:, :TOP_K], moe_w1[m], moe_w3[m], moe_w2[m])
                x = _moe_combine(grp, x, mod5, l, y2, gates, ln_g[l, 1], ln_b[l, 1], alpha)
            if l == n_a - 1:
                k, v, kb, vb = _kv_proj(grp, x, wkv_b)
        return (x, k.reshape(b, t, n_heads, head_dim), v.reshape(b, t, n_heads, head_dim),
                jnp.stack(new_hist, axis=0))

    zero_hist = jnp.zeros((n_a, bp, kw - 1, d), x_prompt.dtype)
    y_p, k_p, v_p, conv_p = trunk(g_prompt, x_prompt, zero_hist, None, None)
    y_s, k_s, v_s, conv_s = trunk(g_sample, x_sample, state_conv, cache_k2, cache_v2)
    return (y_p, y_s, k_p, v_p, conv_p, k_s, v_s, conv_s)
```

```python
import functools

import jax
import jax.numpy as jnp
from jax import lax
from jax.experimental import pallas as pl
from jax.experimental.pallas import tpu as pltpu

LN_EPS = 1e-5
TOP_K = 2
LANES = 128
SUBLANES = 8
VMEM_LIMIT_BYTES = 56 * 1024 * 1024
TAIL_CUTOFF = -110.0

F32 = jnp.float32
BF16 = jnp.bfloat16


def _cparams(n_axes):
    return pltpu.CompilerParams(dimension_semantics=("arbitrary",) * n_axes,
                                vmem_limit_bytes=VMEM_LIMIT_BYTES)


def _dot(a, b):
    return jnp.dot(a, b, preferred_element_type=F32)


def _layer_norm(v, g, b):
    mu = jnp.mean(v, axis=-1, keepdims=True)
    c = v - mu
    var = jnp.mean(c * c, axis=-1, keepdims=True)
    return c * lax.rsqrt(var + LN_EPS) * g + b


def _silu(v):
    return v * jax.nn.sigmoid(v)


def _pick_tile(n, target):
    t = min(n, target)
    while n % t:
        t -= 1
    return t


def _ada_kernel(c_ref, w_ref, b_ref, o_ref):
    ca = _silu(c_ref[...]).astype(BF16)
    o_ref[...] = _dot(ca, w_ref[...].astype(BF16)) + b_ref[...]


def _ada_table(c_all, w_ada, b_ada):
    n_layers, d, d6 = w_ada.shape
    bc = c_all.shape[0]
    tn = _pick_tile(d6, 1536)
    return pl.pallas_call(
        _ada_kernel,
        grid=(n_layers, d6 // tn),
        in_specs=[pl.BlockSpec((bc, d), lambda l, j: (0, 0)),
                  pl.BlockSpec((None, d, tn), lambda l, j: (l, 0, j)),
                  pl.BlockSpec((None, 1, tn), lambda l, j: (l, 0, j))],
        out_specs=pl.BlockSpec((None, bc, tn), lambda l, j: (l, 0, j)),
        out_shape=jax.ShapeDtypeStruct((n_layers, bc, d6), F32),
        compiler_params=_cparams(2),
        name="ada_table",
    )(c_all, w_ada, b_ada.reshape(n_layers, 1, d6))


class _Group:
    def __init__(self, b, t, row_off, rows_target):
        self.b, self.t = b, t
        self.tt = _pick_tile(t, rows_target)
        self.bb = _pick_tile(b, max(1, rows_target // self.tt))
        assert row_off % self.bb == 0
        self.boff = row_off // self.bb
        self.grid = (b // self.bb, t // self.tt)
        self.tm = self.bb * self.tt

    def x_spec(self, d):
        return pl.BlockSpec((self.bb, self.tt, d), lambda b, i: (b, i, 0))

    def mod_spec(self, d, layer, which):
        boff = self.boff
        return pl.BlockSpec((None, self.bb, None, 1, d),
                            lambda b, i: (layer, boff + b, which, 0, 0))


def _const_spec(shape):
    nd = len(shape)
    return pl.BlockSpec(shape, lambda b, i: (0,) * nd)


def _conv_in_kernel(x_ref, sh_ref, sc_ref, w_ref, b_ref, u_ref):
    bb, tt, d = x_ref.shape
    h = x_ref[...] * (1.0 + sc_ref[...]) + sh_ref[...]
    hb = h.reshape(bb * tt, d).astype(BF16)
    a = _dot(hb, w_ref[:, :d]) + b_ref[:, :d]
    g = _dot(hb, w_ref[:, d:]) + b_ref[:, d:]
    u_ref[...] = (a * jax.nn.sigmoid(g)).reshape(bb, tt, d)


def _conv_in(grp, x, mod5, layer, w_in, b_in):
    d = x.shape[-1]
    return pl.pallas_call(
        _conv_in_kernel,
        grid=grp.grid,
        in_specs=[grp.x_spec(d), grp.mod_spec(d, layer, 0), grp.mod_spec(d, layer, 1),
                  _const_spec(w_in.shape), _const_spec((1, 2 * d))],
        out_specs=grp.x_spec(d),
        out_shape=jax.ShapeDtypeStruct(x.shape, F32),
        compiler_params=_cparams(2),
        name="conv_in",
    )(x, mod5, mod5, w_in, b_in.reshape(1, 2 * d))


CONV_CHUNK_ROWS = 64
CONV_CHUNK_LANES = 256


def _conv_out_kernel(*refs, alpha, kw, use_halo):
    if use_halo:
        (u_ref, halo_ref, hist_ref, x_ref, g_ref, wdw_ref, bdw_ref, cg_ref, cb_ref,
         w_ref, b_ref, lg_ref, lb_ref, o_ref, ext_ref, z_ref, win_ref) = refs
    else:
        (u_ref, hist_ref, x_ref, g_ref, wdw_ref, bdw_ref, cg_ref, cb_ref,
         w_ref, b_ref, lg_ref, lb_ref, o_ref, ext_ref, z_ref, win_ref) = refs
    bb, tt, d = u_ref.shape
    hp = ext_ref.shape[1] - tt
    ext_ref[:, hp:, :] = u_ref[...]
    if use_halo:
        first = pl.program_id(1) == 0

        @pl.when(first)
        def _():
            ext_ref[:, :hp, :] = hist_ref[...]

        @pl.when(jnp.logical_not(first))
        def _():
            ext_ref[:, :hp, :] = halo_ref[...]
    else:
        ext_ref[:, :hp, :] = hist_ref[...]

    off = hp - (kw - 1)
    bc, _, cc = win_ref.shape
    rc = min(tt, CONV_CHUNK_ROWS)
    for b0 in range(0, bb, bc):
        for r0 in range(0, tt, rc):
            for c0 in range(0, d, cc):
                acc = jnp.broadcast_to(bdw_ref[:, c0:c0 + cc], (bc, rc, cc))
                for r in range(SUBLANES):
                    taps = [k for k in range(kw) if (off + k) % SUBLANES == r]
                    if not taps:
                        continue
                    span = ((off + taps[-1]) // SUBLANES) * SUBLANES + rc
                    win_ref[:, :span, :] = ext_ref[b0:b0 + bc, r0 + r:r0 + r + span, c0:c0 + cc]
                    for k in taps:
                        a0 = ((off + k) // SUBLANES) * SUBLANES
                        acc = acc + wdw_ref[k, :, c0:c0 + cc] * win_ref[:, a0:a0 + rc, :]
                z_ref[b0:b0 + bc, r0:r0 + rc, c0:c0 + cc] = acc

    z = z_ref[...].reshape(bb * tt, d)
    za = _silu(_layer_norm(z, cg_ref[...], cb_ref[...])).astype(BF16)
    y = (_dot(za, w_ref[...]) + b_ref[...]).reshape(bb, tt, d)
    v = alpha * x_ref[...] + (1.0 + g_ref[...]) * y
    o_ref[...] = _layer_norm(v, lg_ref[...], lb_ref[...])


def _conv_out(grp, u, hist_pad, x, mod5, layer, w_dw8, b_dw, cg, cb, w_out, b_out, lg, lb, alpha):
    d = x.shape[-1]
    kw = w_dw8.shape[0]
    hp = hist_pad.shape[1]
    use_halo = grp.grid[1] > 1
    tt = grp.tt
    rc = min(tt, CONV_CHUNK_ROWS)
    bc = min(grp.bb, CONV_CHUNK_ROWS // rc)
    row = lambda a: a.reshape(1, d)
    in_specs = [grp.x_spec(d)]
    args = [u]
    if use_halo:
        assert grp.bb == 1 and tt % hp == 0
        in_specs.append(pl.BlockSpec((1, hp, d), lambda b, i: (b, jnp.maximum(i * (tt // hp) - 1, 0), 0)))
        args.append(u)
    in_specs += [pl.BlockSpec((grp.bb, hp, d), lambda b, i: (b, 0, 0)),
                 grp.x_spec(d), grp.mod_spec(d, layer, 2),
                 _const_spec(w_dw8.shape), _const_spec((1, d)), _const_spec((1, d)), _const_spec((1, d)),
                 _const_spec(w_out.shape), _const_spec((1, d)), _const_spec((1, d)), _const_spec((1, d))]
    args += [hist_pad, x, mod5, w_dw8, row(b_dw), row(cg), row(cb), w_out, row(b_out), row(lg), row(lb)]
    return pl.pallas_call(
        functools.partial(_conv_out_kernel, alpha=alpha, kw=kw, use_halo=use_halo),
        grid=grp.grid,
        in_specs=in_specs,
        out_specs=grp.x_spec(d),
        out_shape=jax.ShapeDtypeStruct(x.shape, F32),
        scratch_shapes=[pltpu.VMEM((grp.bb, hp + tt, d), F32), pltpu.VMEM((grp.bb, tt, d), F32),
                        pltpu.VMEM((bc, rc + hp, min(d, CONV_CHUNK_LANES)), F32)],
        compiler_params=_cparams(2),
        name="conv_out",
    )(*args)


def _ffn_kernel(x_ref, sh_ref, sc_ref, g_ref, w1_ref, w3_ref, w2_ref, lg_ref, lb_ref, o_ref, *, alpha, fc):
    bb, tt, d = x_ref.shape
    f = w1_ref.shape[1]
    x = x_ref[...]
    hb = (x * (1.0 + sc_ref[...]) + sh_ref[...]).reshape(bb * tt, d).astype(BF16)
    y = jnp.zeros((bb * tt, d), F32)
    for f0 in range(0, f, fc):
        a = _dot(hb, w1_ref[:, f0:f0 + fc])
        b = _dot(hb, w3_ref[:, f0:f0 + fc])
        y = y + _dot((_silu(a) * b).astype(BF16), w2_ref[f0:f0 + fc, :])
    v = alpha * x + (1.0 + g_ref[...]) * y.reshape(bb, tt, d)
    o_ref[...] = _layer_norm(v, lg_ref[...], lb_ref[...])


def _ffn(grp, x, mod5, layer, w1, w3, w2, lg, lb, alpha):
    d = x.shape[-1]
    f = w1.shape[1]
    fc = f // 2 if (f // 2) % LANES == 0 else f
    return pl.pallas_call(
        functools.partial(_ffn_kernel, alpha=alpha, fc=fc),
        grid=grp.grid,
        in_specs=[grp.x_spec(d), grp.mod_spec(d, layer, 3), grp.mod_spec(d, layer, 4), grp.mod_spec(d, layer, 5),
                  _const_spec(w1.shape), _const_spec(w3.shape), _const_spec(w2.shape),
                  _const_spec((1, d)), _const_spec((1, d))],
        out_specs=grp.x_spec(d),
        out_shape=jax.ShapeDtypeStruct(x.shape, F32),
        compiler_params=_cparams(2),
        name="ffn",
    )(x, mod5, mod5, mod5, w1, w3, w2, lg.reshape(1, d), lb.reshape(1, d))


def _router_kernel(x_ref, sh_ref, sc_ref, rw_ref, rb_ref, h_ref, idx_ref, gate_ref, *, n_experts):
    bb, tt, d = x_ref.shape
    h = (x_ref[...] * (1.0 + sc_ref[...]) + sh_ref[...]).reshape(bb * tt, d)
    h_ref[...] = h
    rw = rw_ref[...]
    h_hi = h.astype(BF16)
    h_lo = (h - h_hi.astype(F32)).astype(BF16)
    r_hi = rw.astype(BF16)
    r_lo = (rw - r_hi.astype(F32)).astype(BF16)
    logits = _dot(h_hi, r_hi) + (_dot(h_lo, r_hi) + _dot(h_hi, r_lo)) + rb_ref[...]
    lane = lax.broadcasted_iota(jnp.int32, logits.shape, 1)
    neg = jnp.float32(-jnp.inf)
    lg = jnp.where(lane < n_experts, logits, neg)
    m1 = jnp.max(lg, axis=1, keepdims=True)
    i1 = jnp.min(jnp.where(lg == m1, lane, LANES), axis=1, keepdims=True)
    lg2 = jnp.where(lane == i1, neg, lg)
    m2 = jnp.max(lg2, axis=1, keepdims=True)
    i2 = jnp.min(jnp.where(lg2 == m2, lane, LANES), axis=1, keepdims=True)
    e2 = jnp.exp(m2 - m1)
    den = 1.0 + e2
    idx_ref[...] = jnp.where(lane == 0, i1, jnp.where(lane == 1, i2, 0))
    gate_ref[...] = jnp.where(lane == 0, 1.0 / den, jnp.where(lane == 1, e2 / den, 0.0))


def _router(grp, x, mod5, layer, rw_pad, rb_pad, n_experts):
    d = x.shape[-1]
    n = grp.b * grp.t
    nt = grp.grid[1]
    row_spec = lambda w: pl.BlockSpec((grp.tm, w), lambda b, i: (b * nt + i, 0))
    return pl.pallas_call(
        functools.partial(_router_kernel, n_experts=n_experts),
        grid=grp.grid,
        in_specs=[grp.x_spec(d), grp.mod_spec(d, layer, 3), grp.mod_spec(d, layer, 4),
                  _const_spec(rw_pad.shape), _const_spec((1, LANES))],
        out_specs=[row_spec(d), row_spec(LANES), row_spec(LANES)],
        out_shape=[jax.ShapeDtypeStruct((n, d), F32),
                   jax.ShapeDtypeStruct((n, LANES), jnp.int32),
                   jax.ShapeDtypeStruct((n, LANES), F32)],
        compiler_params=_cparams(2),
        name="router",
    )(x, mod5, mod5, rw_pad, rb_pad)


MOE_ROW_TILE = 1024
MOE_F_TILE = 512
DMA_UNROLL = 8


def _moe_kernel(texp_ref, tval_ref, src_cur_ref, src_nxt_ref, dst_ref, h_hbm,
                w1_ref, w3_ref, w2_ref, out_hbm, xbuf, xb16, acc, stage, gsem, ssem):
    i = pl.program_id(0)
    j = pl.program_id(1)
    nt = pl.num_programs(0)
    nj = pl.num_programs(1)
    tm = xb16.shape[0]
    slot = i % 2

    def gather_copy(row, r, s):
        return pltpu.make_async_copy(h_hbm.at[pl.ds(row, 1)], xbuf.at[s, pl.ds(r, 1)], gsem.at[s])

    def scatter_copy(row, r, s):
        return pltpu.make_async_copy(stage.at[s, pl.ds(r, 1)], out_hbm.at[pl.ds(row, 1)], ssem.at[s])

    def start_gather(src_ref, s):
        def body(r, c):
            gather_copy(src_ref[0, 0, r], r, s).start()
            return c
        lax.fori_loop(0, tm, body, 0, unroll=DMA_UNROLL)

    def wait_gather(s):
        pltpu.make_async_copy(h_hbm.at[pl.ds(0, tm)], xbuf.at[s], gsem.at[s]).wait()

    def wait_scatter(s):
        pltpu.make_async_copy(stage.at[s], out_hbm.at[pl.ds(0, tm)], ssem.at[s]).wait()

    @pl.when(j == 0)
    def _():
        @pl.when(i == 0)
        def _():
            start_gather(src_cur_ref, 0)

        wait_gather(slot)

        @pl.when(i + 1 < nt)
        def _():
            start_gather(src_nxt_ref, 1 - slot)

        xb16[...] = xbuf[slot].astype(BF16)
        acc[...] = jnp.zeros_like(acc)

    @pl.when(tval_ref[i] == 1)
    def _():
        xb = xb16[...]
        a = _dot(xb, w1_ref[...].astype(BF16))
        b = _dot(xb, w3_ref[...].astype(BF16))
        acc[...] += _dot((_silu(a) * b).astype(BF16), w2_ref[...].astype(BF16))

    @pl.when(j == nj - 1)
    def _():
        @pl.when(i >= 2)
        def _():
            wait_scatter(slot)

        stage[slot] = acc[...]

        def body(r, c):
            scatter_copy(dst_ref[0, 0, r], r, slot).start()
            return c
        lax.fori_loop(0, tm, body, 0, unroll=DMA_UNROLL)

        @pl.when(i == nt - 1)
        def _():
            wait_scatter(slot)

            @pl.when(i >= 1)
            def _():
                wait_scatter(1 - slot)


def _moe_plan(idx, n_experts, tm, n_pad):
    n = idx.shape[0]
    m = TOP_K * n
    n_tiles = m // tm + n_experts
    e_flat = idx.T.reshape(m)
    order = jnp.argsort(e_flat, stable=True).astype(jnp.int32)
    counts = jnp.sum(e_flat[:, None] == jnp.arange(n_experts)[None, :], axis=0).astype(jnp.int32)
    padded = ((counts + tm - 1) // tm) * tm
    pend = jnp.cumsum(padded)
    pstart = pend - padded
    ustart = jnp.cumsum(counts) - counts
    tile_first = jnp.arange(n_tiles, dtype=jnp.int32) * tm
    tile_exp = jnp.minimum(jnp.sum(tile_first[:, None] >= pend[None, :], axis=1), n_experts - 1).astype(jnp.int32)
    tile_valid = (tile_first < pend[-1]).astype(jnp.int32)
    p = jnp.arange(n_tiles * tm, dtype=jnp.int32)
    pe = jnp.repeat(tile_exp, tm)
    within = p - pstart[pe]
    real = (within < counts[pe]) & jnp.repeat(tile_valid == 1, tm)
    slot = order[jnp.clip(ustart[pe] + within, 0, m - 1)]
    src = jnp.where(real, slot % n, 0).astype(jnp.int32)
    pad_rank = jnp.cumsum(jnp.logical_not(real).astype(jnp.int32)) - 1
    dst = jnp.where(real, (slot // n) * n_pad + slot % n, TOP_K * n_pad + pad_rank).astype(jnp.int32)
    return (tile_exp, tile_valid, src.reshape(n_tiles, 1, tm), dst.reshape(n_tiles, 1, tm), n_tiles)


def _moe_experts(h, idx, w1, w3, w2, n_pad):
    n, d = h.shape
    n_experts, _, f = w1.shape
    tm = _pick_tile(TOP_K * n, MOE_ROW_TILE)
    tf = _pick_tile(f, MOE_F_TILE)
    tile_exp, tile_valid, src, dst, n_tiles = _moe_plan(idx, n_experts, tm, n_pad)
    out_rows = TOP_K * n_pad + n_experts * tm
    smem_spec = lambda fn: pl.BlockSpec((1, 1, tm), fn, memory_space=pltpu.SMEM)
    grid_spec = pltpu.PrefetchScalarGridSpec(
        num_scalar_prefetch=2,
        grid=(n_tiles, f // tf),
        in_specs=[smem_spec(lambda i, j, te, tv: (i, 0, 0)),
                  smem_spec(lambda i, j, te, tv: (jnp.minimum(i + 1, n_tiles - 1), 0, 0)),
                  smem_spec(lambda i, j, te, tv: (i, 0, 0)),
                  pl.BlockSpec(memory_space=pl.ANY),
                  pl.BlockSpec((None, d, tf), lambda i, j, te, tv: (te[i], 0, j)),
                  pl.BlockSpec((None, d, tf), lambda i, j, te, tv: (te[i], 0, j)),
                  pl.BlockSpec((None, tf, d), lambda i, j, te, tv: (te[i], j, 0))],
        out_specs=pl.BlockSpec(memory_space=pl.ANY),
        scratch_shapes=[pltpu.VMEM((2, tm, d), F32), pltpu.VMEM((tm, d), BF16), pltpu.VMEM((tm, d), F32),
                        pltpu.VMEM((2, tm, d), F32), pltpu.SemaphoreType.DMA((2,)), pltpu.SemaphoreType.DMA((2,))],
    )
    return pl.pallas_call(
        _moe_kernel,
        grid_spec=grid_spec,
        out_shape=jax.ShapeDtypeStruct((out_rows, d), F32),
        compiler_params=_cparams(2),
        name="moe_experts",
    )(tile_exp, tile_valid, src, src, dst, h, w1, w3, w2)


def _combine_kernel(x_ref, g_ref, y0_ref, y1_ref, gate_ref, lg_ref, lb_ref, o_ref, *, alpha):
    bb, tt, d = x_ref.shape
    gt = gate_ref[...]
    y = gt[:, 0:1] * y0_ref[...] + gt[:, 1:2] * y1_ref[...]
    v = alpha * x_ref[...] + (1.0 + g_ref[...]) * y.reshape(bb, tt, d)
    o_ref[...] = _layer_norm(v, lg_ref[...], lb_ref[...])


def _moe_combine(grp, x, mod5, layer, y2, gates, row0, n_pad, lg, lb, alpha):
    d = x.shape[-1]
    nt = grp.grid[1]
    tm = grp.tm
    assert row0 % tm == 0 and n_pad % tm == 0
    k0 = row0 // tm
    k1 = (n_pad + row0) // tm
    return pl.pallas_call(
        functools.partial(_combine_kernel, alpha=alpha),
        grid=grp.grid,
        in_specs=[grp.x_spec(d), grp.mod_spec(d, layer, 5),
                  pl.BlockSpec((tm, d), lambda b, i: (k0 + b * nt + i, 0)),
                  pl.BlockSpec((tm, d), lambda b, i: (k1 + b * nt + i, 0)),
                  pl.BlockSpec((tm, LANES), lambda b, i: (k0 + b * nt + i, 0)),
                  _const_spec((1, d)), _const_spec((1, d))],
        out_specs=grp.x_spec(d),
        out_shape=jax.ShapeDtypeStruct(x.shape, F32),
        compiler_params=_cparams(2),
        name="moe_combine",
    )(x, mod5, y2, y2, gates, lg.reshape(1, d), lb.reshape(1, d))


def _kv_kernel(x_ref, w_ref, k_ref, v_ref, kb_ref, vb_ref):
    bb, tt, d = x_ref.shape
    hd = k_ref.shape[-1]
    xb = x_ref[...].reshape(bb * tt, d).astype(BF16)
    k = _dot(xb, w_ref[:, :hd]).reshape(bb, tt, hd)
    v = _dot(xb, w_ref[:, hd:]).reshape(bb, tt, hd)
    k_ref[...] = k
    v_ref[...] = v
    kb_ref[...] = k.astype(BF16)
    vb_ref[...] = v.astype(BF16)


def _kv_proj(grp, x, w_kv):
    d = x.shape[-1]
    hd = w_kv.shape[1] // 2
    shp = (grp.b, grp.t, hd)
    return pl.pallas_call(
        _kv_kernel,
        grid=grp.grid,
        in_specs=[grp.x_spec(d), _const_spec(w_kv.shape)],
        out_specs=[grp.x_spec(hd)] * 4,
        out_shape=[jax.ShapeDtypeStruct(shp, F32), jax.ShapeDtypeStruct(shp, F32),
                   jax.ShapeDtypeStruct(shp, BF16), jax.ShapeDtypeStruct(shp, BF16)],
        compiler_params=_cparams(2),
        name="kv_proj",
    )(x, w_kv)


def _q_kernel(x_ref, sh_ref, sc_ref, w_ref, q_ref, *, scale):
    bb, tt, d = x_ref.shape
    hb = (x_ref[...] * (1.0 + sc_ref[...]) + sh_ref[...]).reshape(bb * tt, d).astype(BF16)
    q_ref[...] = (_dot(hb, w_ref[...]) * scale).reshape(q_ref.shape).astype(BF16)


def _q_proj(grp, x, mod5, layer, w_q, scale):
    d = x.shape[-1]
    hd = w_q.shape[1]
    return pl.pallas_call(
        functools.partial(_q_kernel, scale=scale),
        grid=grp.grid,
        in_specs=[grp.x_spec(d), grp.mod_spec(d, layer, 0), grp.mod_spec(d, layer, 1), _const_spec(w_q.shape)],
        out_specs=grp.x_spec(hd),
        out_shape=jax.ShapeDtypeStruct((grp.b, grp.t, hd), BF16),
        compiler_params=_cparams(2),
        name="q_proj",
    )(x, mod5, mod5, w_q)


def _o_kernel(o_ref, x_ref, g_ref, w_ref, lg_ref, lb_ref, out_ref, *, alpha):
    bb, tt, d = x_ref.shape
    y = _dot(o_ref[...].reshape(bb * tt, o_ref.shape[-1]).astype(BF16), w_ref[...]).reshape(bb, tt, d)
    v = alpha * x_ref[...] + (1.0 + g_ref[...]) * y
    out_ref[...] = _layer_norm(v, lg_ref[...], lb_ref[...])


def _o_proj(grp, o, x, mod5, layer, w_o, lg, lb, alpha):
    d = x.shape[-1]
    hd = w_o.shape[0]
    return pl.pallas_call(
        functools.partial(_o_kernel, alpha=alpha),
        grid=grp.grid,
        in_specs=[grp.x_spec(hd), grp.x_spec(d), grp.mod_spec(d, layer, 2), _const_spec(w_o.shape),
                  _const_spec((1, d)), _const_spec((1, d))],
        out_specs=grp.x_spec(d),
        out_shape=jax.ShapeDtypeStruct(x.shape, F32),
        compiler_params=_cparams(2),
        name="o_proj",
    )(o, x, mod5, w_o, lg.reshape(1, d), lb.reshape(1, d))


ATTN_Q_BLOCK = 256
ATTN_CACHE_BLOCK = 256
_NT_DIMS = (((1,), (1,)), ((), ()))


def _softplus(z):
    return jnp.maximum(z, 0.0) + jnp.log(1.0 + jnp.exp(-jnp.abs(z)))


def _neg_after(bk):
    jj = lax.broadcasted_iota(jnp.int32, (bk, bk), 0)
    ss = lax.broadcasted_iota(jnp.int32, (bk, bk), 1)
    return jnp.where(jj > ss, -1.0, 0.0).astype(BF16)


def _sb_weights(z, carry, neg_after, valid):
    sp = _softplus(z)
    spm = sp if valid is None else jnp.where(valid, sp, 0.0)
    hi = spm.astype(BF16)
    lo = (spm - hi.astype(F32)).astype(BF16)
    tail = _dot(hi, neg_after) + _dot(lo, neg_after) + carry
    w = jnp.exp(z - sp + tail)
    if valid is not None:
        w = jnp.where(valid, w, 0.0)
    return w, carry - jnp.sum(spm, axis=1, keepdims=True)


def _alive(carries):
    m = carries[0]
    for c in carries[1:]:
        m = jnp.maximum(m, c)
    return jnp.max(m) > TAIL_CUTOFF


def _attn_prompt_kernel(q_ref, kn_ref, vn_ref, o_ref, *, bq, head_dim):
    i = pl.program_id(2)
    w = q_ref.shape[-1]
    nh = w // head_dim
    q = q_ref[...]
    lane = lax.broadcasted_iota(jnp.int32, (bq, w), 1)
    in_head = [(lane >= h * head_dim) & (lane < (h + 1) * head_dim) for h in range(nh)]
    qs = [jnp.where(in_head[h], q, jnp.zeros_like(q)) for h in range(nh)]
    na = _neg_after(bq)
    row = lax.broadcasted_iota(jnp.int32, (bq, bq), 0)
    col = lax.broadcasted_iota(jnp.int32, (bq, bq), 1)
    causal = col < row

    def block(k0, carries, valid):
        kb = kn_ref[pl.ds(k0, bq), :]
        vb = vn_ref[pl.ds(k0, bq), :]
        pvs, cs = [], []
        for h in range(nh):
            z = lax.dot_general(qs[h], kb, _NT_DIMS, preferred_element_type=F32)
            wgt, c = _sb_weights(z, carries[h], na, valid)
            pvs.append(_dot(wgt.astype(BF16), vb))
            cs.append(c)
        return pvs, cs

    zero_c = [jnp.zeros((bq, 1), F32)] * nh
    pv0, c0 = block(pl.multiple_of(i * bq, bq), zero_c, causal)
    has_prev = i > 0
    pv1, c1 = block(pl.multiple_of(jnp.maximum(i - 1, 0) * bq, bq), c0, None)
    acc = jnp.zeros((bq, w), F32)
    carries = []
    for h in range(nh):
        acc = jnp.where(in_head[h], pv0[h] + jnp.where(has_prev, pv1[h], 0.0), acc)
        carries.append(jnp.where(has_prev, c1[h], c0[h]))

    def cond(st):
        return (st[0] >= 0) & _alive(st[1])

    def body(st):
        jb, cs, ac = st
        pvs, cs = block(pl.multiple_of(jb * bq, bq), cs, None)
        for h in range(nh):
            ac = jnp.where(in_head[h], ac + pvs[h], ac)
        return jb - 1, tuple(cs), ac

    _, _, acc = lax.while_loop(cond, body, (i - 2, tuple(carries), acc))
    o_ref[...] = acc.astype(o_ref.dtype)


def _attention_prompt(q, kn, vn, head_dim):
    b, t, hd = q.shape
    w = LANES if hd % LANES == 0 else hd
    bq = _pick_tile(t, ATTN_Q_BLOCK)
    return pl.pallas_call(
        functools.partial(_attn_prompt_kernel, bq=bq, head_dim=head_dim),
        grid=(b, hd // w, t // bq),
        in_specs=[pl.BlockSpec((None, bq, w), lambda b_, h_, i: (b_, i, h_)),
                  pl.BlockSpec((None, t, w), lambda b_, h_, i: (b_, 0, h_)),
                  pl.BlockSpec((None, t, w), lambda b_, h_, i: (b_, 0, h_))],
        out_specs=pl.BlockSpec((None, bq, w), lambda b_, h_, i: (b_, i, h_)),
        out_shape=jax.ShapeDtypeStruct((b, t, hd), BF16),
        compiler_params=_cparams(3),
        name="sb_attention_prompt",
    )(q, kn, vn)


def _attn_sample_kernel(q_ref, kn_ref, vn_ref, ck_hbm, cv_hbm, o_ref, kbuf, vbuf, sem, *, bk):
    b = pl.program_id(0)
    nb = pl.num_programs(0)
    n_heads, t, _ = q_ref.shape
    rows = bk * n_heads
    ncb = ck_hbm.shape[1] // rows

    def copies(bi, blk, slot):
        r0 = pl.multiple_of(blk * rows, rows)
        return (pltpu.make_async_copy(ck_hbm.at[bi, pl.ds(r0, rows)], kbuf.at[slot], sem.at[0, slot]),
                pltpu.make_async_copy(cv_hbm.at[bi, pl.ds(r0, rows)], vbuf.at[slot], sem.at[1, slot]))

    def start(bi, blk, slot):
        for c in copies(bi, blk, slot):
            c.start()

    def wait(bi, blk, slot):
        for c in copies(bi, blk, slot):
            c.wait()

    slot = b % 2

    @pl.when(b == 0)
    def _():
        start(0, ncb - 1, 0)

    @pl.when(b + 1 < nb)
    def _():
        start(b + 1, ncb - 1, 1 - slot)

    def head_rows(x, h):
        return x[h * t:(h + 1) * t]

    row = lax.broadcasted_iota(jnp.int32, (n_heads * t, t), 0)
    col = lax.broadcasted_iota(jnp.int32, (n_heads * t, t), 1)
    causal = col < lax.rem(row, t)
    z = jnp.concatenate([lax.dot_general(q_ref[h], kn_ref[h], _NT_DIMS, preferred_element_type=F32)
                         for h in range(n_heads)], axis=0)
    wgt, carry = _sb_weights(z, jnp.zeros((n_heads * t, 1), F32), _neg_after(t), causal)
    wgt = wgt.astype(BF16)
    acc = jnp.concatenate([_dot(head_rows(wgt, h), vn_ref[h]) for h in range(n_heads)], axis=0)

    na = _neg_after(bk)

    def cache_block(s, carry, acc):
        z = jnp.concatenate(
            [lax.dot_general(q_ref[h], kbuf[s, pl.ds(h, bk, stride=n_heads), :].astype(BF16), _NT_DIMS,
                             preferred_element_type=F32) for h in range(n_heads)], axis=0)
        wgt, carry = _sb_weights(z, carry, na, None)
        wgt = wgt.astype(BF16)
        pv = jnp.concatenate(
            [_dot(head_rows(wgt, h), vbuf[s, pl.ds(h, bk, stride=n_heads), :].astype(BF16))
             for h in range(n_heads)], axis=0)
        return carry, acc + pv

    wait(b, ncb - 1, slot)
    carry, acc = cache_block(slot, carry, acc)

    def cond(st):
        return (st[0] >= 0) & (jnp.max(st[1]) > TAIL_CUTOFF)

    def body(st):
        jb, c, a = st
        start(b, jb, 2)
        wait(b, jb, 2)
        c, a = cache_block(2, c, a)
        return jb - 1, c, a

    _, _, acc = lax.while_loop(cond, body, (ncb - 2, carry, acc))
    for h in range(n_heads):
        o_ref[h] = head_rows(acc, h)


def _attention_sample(q4, kn4, vn4, cache_k3, cache_v3):
    b, n_heads, t, dh = q4.shape
    p = cache_k3.shape[1] // n_heads
    bk = _pick_tile(p, ATTN_CACHE_BLOCK)
    head_spec = pl.BlockSpec((None, n_heads, t, dh), lambda b_: (b_, 0, 0, 0))
    return pl.pallas_call(
        functools.partial(_attn_sample_kernel, bk=bk),
        grid=(b,),
        in_specs=[head_spec, head_spec, head_spec,
                  pl.BlockSpec(memory_space=pl.ANY), pl.BlockSpec(memory_space=pl.ANY)],
        out_specs=head_spec,
        out_shape=jax.ShapeDtypeStruct((b, n_heads, t, dh), F32),
        scratch_shapes=[pltpu.VMEM((3, bk * n_heads, dh), F32), pltpu.VMEM((3, bk * n_heads, dh), F32),
                        pltpu.SemaphoreType.DMA((2, 3))],
        compiler_params=_cparams(1),
        name="sb_attention_sample",
    )(q4, kn4, vn4, cache_k3, cache_v3)


ROWS_PER_STEP = 512
CONV_CONTEXT_ROWS = 32


def kernel(x_prompt, x_sample, cache_k, cache_v, state_conv, c_prompt, c_sample, w_ada, b_ada, ln_g, ln_b, conv_w_in, conv_b_in, conv_w_dw, conv_b_dw, conv_ln_g, conv_ln_b, conv_w_out, conv_b_out, w_kv, w_q, w_o, ffn_w1, ffn_w3, ffn_w2, moe_router, moe_router_b, moe_w1, moe_w3, moe_w2):
    depth, d, _ = w_ada.shape
    n_a = conv_w_in.shape[0]
    kw = conv_w_dw.shape[1]
    n_heads, head_dim = cache_k.shape[2], cache_k.shape[3]
    hd = n_heads * head_dim
    n_experts = moe_router.shape[-1]
    alpha = float((2 * depth) ** 0.25)
    bp, tp, _ = x_prompt.shape
    bs, ts, _ = x_sample.shape
    past = cache_k.shape[1]
    hp = CONV_CONTEXT_ROWS
    assert kw - 1 <= hp and n_experts <= LANES

    c_all = jnp.concatenate([c_sample, c_prompt], axis=0)
    mod = _ada_table(c_all, w_ada, b_ada)
    mod5 = mod.reshape(depth, bs + bp, 6, 1, d)

    g_prompt = _Group(bp, tp, bs, ROWS_PER_STEP)
    g_sample = _Group(bs, ts, 0, ROWS_PER_STEP)

    cw_in, cw_out = conv_w_in.astype(BF16), conv_w_out.astype(BF16)
    wkv_b, wq_b, wo_b = w_kv.astype(BF16), w_q.astype(BF16), w_o.astype(BF16)
    f1_b, f3_b, f2_b = ffn_w1.astype(BF16), ffn_w3.astype(BF16), ffn_w2.astype(BF16)
    w_dw8 = conv_w_dw.reshape(n_a, kw, 1, d)
    rw_pad = jnp.pad(moe_router, ((0, 0), (0, 0), (0, LANES - n_experts)))
    rb_pad = jnp.pad(moe_router_b, ((0, 0), (0, LANES - n_experts))).reshape(-1, 1, LANES)
    cache_k2 = cache_k.reshape(bs, past * n_heads, head_dim)
    cache_v2 = cache_v.reshape(bs, past * n_heads, head_dim)

    groups = [dict(grp=g_prompt, x=x_prompt, hist=jnp.zeros((n_a, bp, kw - 1, d), x_prompt.dtype),
                   cache=None, row0=0),
              dict(grp=g_sample, x=x_sample, hist=state_conv, cache=(cache_k2, cache_v2), row0=bp * tp)]
    n_all = bp * tp + bs * ts
    n_pad = -(-n_all // ROWS_PER_STEP) * ROWS_PER_STEP
    for g in groups:
        g["hist_pad"] = jnp.pad(g["hist"], ((0, 0), (0, 0), (hp - (kw - 1), 0), (0, 0)))
        g["new_hist"] = []

    def heads_major(a):
        return a.reshape(a.shape[0], a.shape[1], n_heads, head_dim).transpose(0, 2, 1, 3)

    for l in range(depth):
        m = l // 2
        for g in groups:
            grp, x = g["grp"], g["x"]
            if l < n_a:
                t = grp.t
                u = _conv_in(grp, x, mod5, l, cw_in[l], conv_b_in[l])
                x = _conv_out(grp, u, g["hist_pad"][l], x, mod5, l, w_dw8[l], conv_b_dw[l], conv_ln_g[l],
                              conv_ln_b[l], cw_out[l], conv_b_out[l], ln_g[l, 0], ln_b[l, 0], alpha)
                u_ext = jnp.concatenate([g["hist"][l][:, max(0, kw - 1 - t):], u[:, max(0, t - (kw - 1)):]], axis=1)
                g["new_hist"].append(u_ext[:, -(kw - 1):])
            else:
                j = l - n_a
                q = _q_proj(grp, x, mod5, l, wq_b[j], head_dim ** -0.5)
                if g["cache"] is None:
                    o = _attention_prompt(q, g["kb"], g["vb"], head_dim)
                else:
                    o4 = _attention_sample(heads_major(q), g["kb4"], g["vb4"], *g["cache"])
                    o = o4.transpose(0, 2, 1, 3).reshape(grp.b, grp.t, hd)
                x = _o_proj(grp, o, x, mod5, l, wo_b[j], ln_g[l, 0], ln_b[l, 0], alpha)
            if l % 2 == 0:
                x = _ffn(grp, x, mod5, l, f1_b[m], f3_b[m], f2_b[m], ln_g[l, 1], ln_b[l, 1], alpha)
            g["x"] = x
        if l % 2 == 1:
            routed = [_router(g["grp"], g["x"], mod5, l, rw_pad[m], rb_pad[m], n_experts) for g in groups]
            h_all, idx_all, gates_all = (jnp.concatenate(parts, axis=0) for parts in zip(*routed))
            y2 = _moe_experts(h_all, idx_all[:, :TOP_K], moe_w1[m], moe_w3[m], moe_w2[m], n_pad)
            for g in groups:
                g["x"] = _moe_combine(g["grp"], g["x"], mod5, l, y2, gates_all, g["row0"], n_pad,
                                      ln_g[l, 1], ln_b[l, 1], alpha)
        if l == n_a - 1:
            for g in groups:
                g["k"], g["v"], g["kb"], g["vb"] = _kv_proj(g["grp"], g["x"], wkv_b)
                if g["cache"] is not None:
                    g["kb4"], g["vb4"] = heads_major(g["kb"]), heads_major(g["vb"])

    def outputs(g):
        shp = (g["grp"].b, g["grp"].t, n_heads, head_dim)
        return g["x"], g["k"].reshape(shp), g["v"].reshape(shp), jnp.stack(g["new_hist"], axis=0)

    y_p, k_p, v_p, conv_p = outputs(groups[0])
    y_s, k_s, v_s, conv_s = outputs(groups[1])
    return (y_p, y_s, k_p, v_p, conv_p, k_s, v_s, conv_s)
```

```python
import functools

import jax
import jax.numpy as jnp
from jax import lax
from jax.experimental import pallas as pl
from jax.experimental.pallas import tpu as pltpu

LN_EPS = 1e-5
TOP_K = 2
LANES = 128
SUBLANES = 8
VMEM_LIMIT_BYTES = 56 * 1024 * 1024
TAIL_CUTOFF = -110.0

F32 = jnp.float32
BF16 = jnp.bfloat16


def _cparams(n_axes):
    return pltpu.CompilerParams(dimension_semantics=("arbitrary",) * n_axes,
                                vmem_limit_bytes=VMEM_LIMIT_BYTES)


def _dot(a, b):
    return jnp.dot(a, b, preferred_element_type=F32)


def _layer_norm(v, g, b):
    mu = jnp.mean(v, axis=-1, keepdims=True)
    c = v - mu
    var = jnp.mean(c * c, axis=-1, keepdims=True)
    return c * lax.rsqrt(var + LN_EPS) * g + b


def _silu(v):
    return v * jax.nn.sigmoid(v)


def _pick_tile(n, target):
    t = min(n, target)
    while n % t:
        t -= 1
    return t


def _ada_kernel(c_ref, w_ref, b_ref, o_ref):
    ca = _silu(c_ref[...]).astype(BF16)
    o_ref[...] = _dot(ca, w_ref[...].astype(BF16)) + b_ref[...]


def _ada_table(c_all, w_ada, b_ada):
    n_layers, d, d6 = w_ada.shape
    bc = c_all.shape[0]
    tn = _pick_tile(d6, 1536)
    return pl.pallas_call(
        _ada_kernel,
        grid=(n_layers, d6 // tn),
        in_specs=[pl.BlockSpec((bc, d), lambda l, j: (0, 0)),
                  pl.BlockSpec((None, d, tn), lambda l, j: (l, 0, j)),
                  pl.BlockSpec((None, 1, tn), lambda l, j: (l, 0, j))],
        out_specs=pl.BlockSpec((None, bc, tn), lambda l, j: (l, 0, j)),
        out_shape=jax.ShapeDtypeStruct((n_layers, bc, d6), F32),
        compiler_params=_cparams(2),
        name="ada_table",
    )(c_all, w_ada, b_ada.reshape(n_layers, 1, d6))


class _Group:
    def __init__(self, b, t, row_off, rows_target):
        self.b, self.t = b, t
        self.tt = _pick_tile(t, rows_target)
        self.bb = _pick_tile(b, max(1, rows_target // self.tt))
        assert row_off % self.bb == 0
        self.boff = row_off // self.bb
        self.grid = (b // self.bb, t // self.tt)
        self.tm = self.bb * self.tt

    def x_spec(self, d):
        return pl.BlockSpec((self.bb, self.tt, d), lambda b, i: (b, i, 0))

    def mod_spec(self, d, layer, which):
        boff = self.boff
        return pl.BlockSpec((None, self.bb, None, 1, d),
                            lambda b, i: (layer, boff + b, which, 0, 0))


def _const_spec(shape):
    nd = len(shape)
    return pl.BlockSpec(shape, lambda b, i: (0,) * nd)


def _conv_in_kernel(x_ref, sh_ref, sc_ref, w_ref, b_ref, u_ref):
    bb, tt, d = x_ref.shape
    h = x_ref[...] * (1.0 + sc_ref[...]) + sh_ref[...]
    hb = h.reshape(bb * tt, d).astype(BF16)
    a = _dot(hb, w_ref[:, :d]) + b_ref[:, :d]
    g = _dot(hb, w_ref[:, d:]) + b_ref[:, d:]
    u_ref[...] = (a * jax.nn.sigmoid(g)).reshape(bb, tt, d)


def _conv_in(grp, x, mod5, layer, w_in, b_in):
    d = x.shape[-1]
    return pl.pallas_call(
        _conv_in_kernel,
        grid=grp.grid,
        in_specs=[grp.x_spec(d), grp.mod_spec(d, layer, 0), grp.mod_spec(d, layer, 1),
                  _const_spec(w_in.shape), _const_spec((1, 2 * d))],
        out_specs=grp.x_spec(d),
        out_shape=jax.ShapeDtypeStruct(x.shape, F32),
        compiler_params=_cparams(2),
        name="conv_in",
    )(x, mod5, mod5, w_in, b_in.reshape(1, 2 * d))


CONV_CHUNK_ROWS = 64
CONV_CHUNK_LANES = 256


def _conv_out_kernel(*refs, alpha, kw, use_halo):
    if use_halo:
        (u_ref, halo_ref, hist_ref, x_ref, g_ref, wdw_ref, bdw_ref, cg_ref, cb_ref,
         w_ref, b_ref, lg_ref, lb_ref, o_ref, ext_ref, z_ref, win_ref) = refs
    else:
        (u_ref, hist_ref, x_ref, g_ref, wdw_ref, bdw_ref, cg_ref, cb_ref,
         w_ref, b_ref, lg_ref, lb_ref, o_ref, ext_ref, z_ref, win_ref) = refs
    bb, tt, d = u_ref.shape
    hp = ext_ref.shape[1] - tt
    ext_ref[:, hp:, :] = u_ref[...]
    if use_halo:
        first = pl.program_id(1) == 0

        @pl.when(first)
        def _():
            ext_ref[:, :hp, :] = hist_ref[...]

        @pl.when(jnp.logical_not(first))
        def _():
            ext_ref[:, :hp, :] = halo_ref[...]
    else:
        ext_ref[:, :hp, :] = hist_ref[...]

    off = hp - (kw - 1)
    bc, _, cc = win_ref.shape
    rc = min(tt, CONV_CHUNK_ROWS)
    for b0 in range(0, bb, bc):
        for r0 in range(0, tt, rc):
            for c0 in range(0, d, cc):
                acc = jnp.broadcast_to(bdw_ref[:, c0:c0 + cc], (bc, rc, cc))
                for r in range(SUBLANES):
                    taps = [k for k in range(kw) if (off + k) % SUBLANES == r]
                    if not taps:
                        continue
                    span = ((off + taps[-1]) // SUBLANES) * SUBLANES + rc
                    win_ref[:, :span, :] = ext_ref[b0:b0 + bc, r0 + r:r0 + r + span, c0:c0 + cc]
                    for k in taps:
                        a0 = ((off + k) // SUBLANES) * SUBLANES
                        acc = acc + wdw_ref[k, :, c0:c0 + cc] * win_ref[:, a0:a0 + rc, :]
                z_ref[b0:b0 + bc, r0:r0 + rc, c0:c0 + cc] = acc

    z = z_ref[...].reshape(bb * tt, d)
    za = _silu(_layer_norm(z, cg_ref[...], cb_ref[...])).astype(BF16)
    y = (_dot(za, w_ref[...]) + b_ref[...]).reshape(bb, tt, d)
    v = alpha * x_ref[...] + (1.0 + g_ref[...]) * y
    o_ref[...] = _layer_norm(v, lg_ref[...], lb_ref[...])


def _conv_out(grp, u, hist_pad, x, mod5, layer, w_dw8, b_dw, cg, cb, w_out, b_out, lg, lb, alpha):
    d = x.shape[-1]
    kw = w_dw8.shape[0]
    hp = hist_pad.shape[1]
    use_halo = grp.grid[1] > 1
    tt = grp.tt
    rc = min(tt, CONV_CHUNK_ROWS)
    bc = min(grp.bb, CONV_CHUNK_ROWS // rc)
    row = lambda a: a.reshape(1, d)
    in_specs = [grp.x_spec(d)]
    args = [u]
    if use_halo:
        assert grp.bb == 1 and tt % hp == 0
        in_specs.append(pl.BlockSpec((1, hp, d), lambda b, i: (b, jnp.maximum(i * (tt // hp) - 1, 0), 0)))
        args.append(u)
    in_specs += [pl.BlockSpec((grp.bb, hp, d), lambda b, i: (b, 0, 0)),
                 grp.x_spec(d), grp.mod_spec(d, layer, 2),
                 _const_spec(w_dw8.shape), _const_spec((1, d)), _const_spec((1, d)), _const_spec((1, d)),
                 _const_spec(w_out.shape), _const_spec((1, d)), _const_spec((1, d)), _const_spec((1, d))]
    args += [hist_pad, x, mod5, w_dw8, row(b_dw), row(cg), row(cb), w_out, row(b_out), row(lg), row(lb)]
    return pl.pallas_call(
        functools.partial(_conv_out_kernel, alpha=alpha, kw=kw, use_halo=use_halo),
        grid=grp.grid,
        in_specs=in_specs,
        out_specs=grp.x_spec(d),
        out_shape=jax.ShapeDtypeStruct(x.shape, F32),
        scratch_shapes=[pltpu.VMEM((grp.bb, hp + tt, d), F32), pltpu.VMEM((grp.bb, tt, d), F32),
                        pltpu.VMEM((bc, rc + hp, min(d, CONV_CHUNK_LANES)), F32)],
        compiler_params=_cparams(2),
        name="conv_out",
    )(*args)


def _ffn_kernel(x_ref, sh_ref, sc_ref, g_ref, w1_ref, w3_ref, w2_ref, lg_ref, lb_ref, o_ref, *, alpha, fc):
    bb, tt, d = x_ref.shape
    f = w1_ref.shape[1]
    x = x_ref[...]
    hb = (x * (1.0 + sc_ref[...]) + sh_ref[...]).reshape(bb * tt, d).astype(BF16)
    y = jnp.zeros((bb * tt, d), F32)
    for f0 in range(0, f, fc):
        a = _dot(hb, w1_ref[:, f0:f0 + fc])
        b = _dot(hb, w3_ref[:, f0:f0 + fc])
        y = y + _dot((_silu(a) * b).astype(BF16), w2_ref[f0:f0 + fc, :])
    v = alpha * x + (1.0 + g_ref[...]) * y.reshape(bb, tt, d)
    o_ref[...] = _layer_norm(v, lg_ref[...], lb_ref[...])


def _ffn(grp, x, mod5, layer, w1, w3, w2, lg, lb, alpha):
    d = x.shape[-1]
    f = w1.shape[1]
    fc = f // 2 if (f // 2) % LANES == 0 else f
    return pl.pallas_call(
        functools.partial(_ffn_kernel, alpha=alpha, fc=fc),
        grid=grp.grid,
        in_specs=[grp.x_spec(d), grp.mod_spec(d, layer, 3), grp.mod_spec(d, layer, 4), grp.mod_spec(d, layer, 5),
                  _const_spec(w1.shape), _const_spec(w3.shape), _const_spec(w2.shape),
                  _const_spec((1, d)), _const_spec((1, d))],
        out_specs=grp.x_spec(d),
        out_shape=jax.ShapeDtypeStruct(x.shape, F32),
        compiler_params=_cparams(2),
        name="ffn",
    )(x, mod5, mod5, mod5, w1, w3, w2, lg.reshape(1, d), lb.reshape(1, d))


def _router_kernel(x_ref, sh_ref, sc_ref, rw_ref, rb_ref, h_ref, idx_ref, gate_ref, *, n_experts):
    bb, tt, d = x_ref.shape
    h = (x_ref[...] * (1.0 + sc_ref[...]) + sh_ref[...]).reshape(bb * tt, d)
    h_ref[...] = h
    rw = rw_ref[...]
    h_hi = h.astype(BF16)
    h_lo = (h - h_hi.astype(F32)).astype(BF16)
    r_hi = rw.astype(BF16)
    r_lo = (rw - r_hi.astype(F32)).astype(BF16)
    logits = _dot(h_hi, r_hi) + (_dot(h_lo, r_hi) + _dot(h_hi, r_lo)) + rb_ref[...]
    lane = lax.broadcasted_iota(jnp.int32, logits.shape, 1)
    neg = jnp.float32(-jnp.inf)
    lg = jnp.where(lane < n_experts, logits, neg)
    m1 = jnp.max(lg, axis=1, keepdims=True)
    i1 = jnp.min(jnp.where(lg == m1, lane, LANES), axis=1, keepdims=True)
    lg2 = jnp.where(lane == i1, neg, lg)
    m2 = jnp.max(lg2, axis=1, keepdims=True)
    i2 = jnp.min(jnp.where(lg2 == m2, lane, LANES), axis=1, keepdims=True)
    e2 = jnp.exp(m2 - m1)
    den = 1.0 + e2
    idx_ref[...] = jnp.where(lane == 0, i1, jnp.where(lane == 1, i2, 0))
    gate_ref[...] = jnp.where(lane == 0, 1.0 / den, jnp.where(lane == 1, e2 / den, 0.0))


def _router(grp, x, mod5, layer, rw_pad, rb_pad, n_experts):
    d = x.shape[-1]
    n = grp.b * grp.t
    nt = grp.grid[1]
    row_spec = lambda w: pl.BlockSpec((grp.tm, w), lambda b, i: (b * nt + i, 0))
    return pl.pallas_call(
        functools.partial(_router_kernel, n_experts=n_experts),
        grid=grp.grid,
        in_specs=[grp.x_spec(d), grp.mod_spec(d, layer, 3), grp.mod_spec(d, layer, 4),
                  _const_spec(rw_pad.shape), _const_spec((1, LANES))],
        out_specs=[row_spec(d), row_spec(LANES), row_spec(LANES)],
        out_shape=[jax.ShapeDtypeStruct((n, d), F32),
                   jax.ShapeDtypeStruct((n, LANES), jnp.int32),
                   jax.ShapeDtypeStruct((n, LANES), F32)],
        compiler_params=_cparams(2),
        name="router",
    )(x, mod5, mod5, rw_pad, rb_pad)


MOE_ROW_TILE = 896
MOE_F_TILE = 512
DMA_UNROLL = 8
MOE_ISSUE_POINTS = 4


def _moe_kernel(texp_ref, tval_ref, src_cur_ref, src_nxt_ref, dst_cur_ref, dst_prv_ref, h_hbm,
                w1_ref, w3_ref, w2_ref, out_hbm, xbuf, xb16, acc, stage, gsem, ssem, *, n_f_steps, dump_row0):
    i = pl.program_id(0)
    j = pl.program_id(1)
    nt = pl.num_programs(0)
    tm = xb16.shape[0]
    rc = tm // n_f_steps
    slot = i % 2
    other = 1 - slot
    valid = tval_ref[i] == 1
    prev_valid = (i > 0) & (tval_ref[jnp.maximum(i - 1, 0)] == 1)
    next_valid = (i + 1 < nt) & (tval_ref[jnp.minimum(i + 1, nt - 1)] == 1)

    def gather_copy(row, r, s):
        return pltpu.make_async_copy(h_hbm.at[pl.ds(row, 1)], xbuf.at[s, pl.ds(r, 1)], gsem.at[s])

    def scatter_copy(row, r, s):
        return pltpu.make_async_copy(stage.at[s, pl.ds(r, 1)], out_hbm.at[pl.ds(row, 1)], ssem.at[s])

    def loop_start(copy_fn, idx_ref, s):
        def body(r, c):
            copy_fn(idx_ref[0, 0, r], r, s).start()
            return c
        lax.fori_loop(0, tm, body, 0, unroll=DMA_UNROLL)

    def wait_gather(s):
        pltpu.make_async_copy(h_hbm.at[pl.ds(0, tm)], xbuf.at[s], gsem.at[s]).wait()

    def wait_scatter(s):
        pltpu.make_async_copy(stage.at[s], out_hbm.at[pl.ds(0, tm)], ssem.at[s]).wait()

    @pl.when(j == 0)
    def _():
        @pl.when(i == 0)
        def _():
            loop_start(gather_copy, src_cur_ref, 0)
            stage[1] = jnp.zeros(stage.shape[1:], stage.dtype)

        @pl.when(valid | prev_valid)
        def _():
            wait_gather(slot)

        @pl.when(valid)
        def _():
            xb16[...] = xbuf[slot].astype(BF16)
            acc[...] = jnp.zeros_like(acc)

        @pl.when(jnp.logical_not(valid))
        def _():
            acc[...] = jnp.zeros_like(acc)
            row0 = pl.multiple_of(dst_cur_ref[0, 0, 0], SUBLANES)
            fill = pltpu.make_async_copy(acc, out_hbm.at[pl.ds(row0, tm)], ssem.at[slot])
            fill.start()
            fill.wait()

    @pl.when(valid)
    def _():
        base = j * rc
        bounds = [(p * rc) // MOE_ISSUE_POINTS for p in range(MOE_ISSUE_POINTS + 1)]

        def issue(part):
            for r in range(bounds[part], bounds[part + 1]):
                gather_copy(src_nxt_ref[0, 0, base + r], base + r, other).start()
            for r in range(bounds[part], bounds[part + 1]):
                row = jnp.where(i > 0, dst_prv_ref[0, 0, base + r], dump_row0 + base + r)
                scatter_copy(row, base + r, other).start()

        xb = xb16[...]
        issue(0)
        a = _dot(xb, w1_ref[...])
        issue(1)
        b = _dot(xb, w3_ref[...])
        issue(2)
        acc[...] += _dot((_silu(a) * b).astype(BF16), w2_ref[...])
        issue(3)

    @pl.when(valid & (j == n_f_steps - 1))
    def _():
        wait_scatter(other)
        stage[slot] = acc[...]

        @pl.when(jnp.logical_not(next_valid))
        def _():
            loop_start(scatter_copy, dst_cur_ref, slot)
            wait_scatter(slot)

            @pl.when(i == nt - 1)
            def _():
                wait_gather(other)


def _moe_plan(idx, n_experts, tm, n_pad):
    n = idx.shape[0]
    m = TOP_K * n
    n_tiles = -(-m // tm) + n_experts
    e_flat = idx.T.reshape(m)
    order = jnp.argsort(e_flat, stable=True).astype(jnp.int32)
    counts = jnp.sum(e_flat[:, None] == jnp.arange(n_experts)[None, :], axis=0).astype(jnp.int32)
    padded = ((counts + tm - 1) // tm) * tm
    pend = jnp.cumsum(padded)
    pstart = pend - padded
    ustart = jnp.cumsum(counts) - counts
    tile_first = jnp.arange(n_tiles, dtype=jnp.int32) * tm
    tile_exp = jnp.minimum(jnp.sum(tile_first[:, None] >= pend[None, :], axis=1), n_experts - 1).astype(jnp.int32)
    tile_valid = (tile_first < pend[-1]).astype(jnp.int32)
    p = jnp.arange(n_tiles * tm, dtype=jnp.int32)
    pe = jnp.repeat(tile_exp, tm)
    within = p - pstart[pe]
    real = (within < counts[pe]) & jnp.repeat(tile_valid == 1, tm)
    slot = order[jnp.clip(ustart[pe] + within, 0, m - 1)]
    src = jnp.where(real, slot % n, 0).astype(jnp.int32)
    pad_rank = jnp.cumsum(jnp.logical_not(real).astype(jnp.int32)) - 1
    dst = jnp.where(real, (slot // n) * n_pad + slot % n, TOP_K * n_pad + pad_rank).astype(jnp.int32)
    return (tile_exp, tile_valid, src.reshape(n_tiles, 1, tm), dst.reshape(n_tiles, 1, tm), n_tiles)


def _moe_experts(h, idx, w1, w3, w2, lm, n_pad):
    n, d = h.shape
    _, n_experts, _, f = w1.shape
    tf = _pick_tile(f, MOE_F_TILE)
    n_f_steps = f // tf
    tm = (MOE_ROW_TILE // n_f_steps) * n_f_steps
    assert tm % SUBLANES == 0 and (TOP_K * n) % SUBLANES == 0 and n_pad % SUBLANES == 0
    tile_exp, tile_valid, src, dst, n_tiles = _moe_plan(idx, n_experts, tm, n_pad)
    dump_row0 = TOP_K * n_pad + (n_tiles * tm - TOP_K * n)
    out_rows = dump_row0 + tm
    smem_spec = lambda fn: pl.BlockSpec((1, 1, tm), fn, memory_space=pltpu.SMEM)
    grid_spec = pltpu.PrefetchScalarGridSpec(
        num_scalar_prefetch=2,
        grid=(n_tiles, n_f_steps),
        in_specs=[smem_spec(lambda i, j, te, tv: (i, 0, 0)),
                  smem_spec(lambda i, j, te, tv: (jnp.minimum(i + 1, n_tiles - 1), 0, 0)),
                  smem_spec(lambda i, j, te, tv: (i, 0, 0)),
                  smem_spec(lambda i, j, te, tv: (jnp.maximum(i - 1, 0), 0, 0)),
                  pl.BlockSpec(memory_space=pl.ANY),
                  pl.BlockSpec((None, None, d, tf), lambda i, j, te, tv: (lm, te[i], 0, j)),
                  pl.BlockSpec((None, None, d, tf), lambda i, j, te, tv: (lm, te[i], 0, j)),
                  pl.BlockSpec((None, None, tf, d), lambda i, j, te, tv: (lm, te[i], j, 0))],
        out_specs=pl.BlockSpec(memory_space=pl.ANY),
        scratch_shapes=[pltpu.VMEM((2, tm, d), F32), pltpu.VMEM((tm, d), BF16), pltpu.VMEM((tm, d), F32),
                        pltpu.VMEM((2, tm, d), F32), pltpu.SemaphoreType.DMA((2,)), pltpu.SemaphoreType.DMA((2,))],
    )
    return pl.pallas_call(
        functools.partial(_moe_kernel, n_f_steps=n_f_steps, dump_row0=dump_row0),
        grid_spec=grid_spec,
        out_shape=jax.ShapeDtypeStruct((out_rows, d), F32),
        compiler_params=_cparams(2),
        name="moe_experts",
    )(tile_exp, tile_valid, src, src, dst, dst, h, w1, w3, w2)


def _combine_kernel(x_ref, g_ref, y0_ref, y1_ref, gate_ref, lg_ref, lb_ref, o_ref, *, alpha):
    bb, tt, d = x_ref.shape
    gt = gate_ref[...]
    y = gt[:, 0:1] * y0_ref[...] + gt[:, 1:2] * y1_ref[...]
    v = alpha * x_ref[...] + (1.0 + g_ref[...]) * y.reshape(bb, tt, d)
    o_ref[...] = _layer_norm(v, lg_ref[...], lb_ref[...])


def _moe_combine(grp, x, mod5, layer, y2, gates, row0, n_pad, lg, lb, alpha):
    d = x.shape[-1]
    nt = grp.grid[1]
    tm = grp.tm
    assert row0 % tm == 0 and n_pad % tm == 0
    k0 = row0 // tm
    k1 = (n_pad + row0) // tm
    return pl.pallas_call(
        functools.partial(_combine_kernel, alpha=alpha),
        grid=grp.grid,
        in_specs=[grp.x_spec(d), grp.mod_spec(d, layer, 5),
                  pl.BlockSpec((tm, d), lambda b, i: (k0 + b * nt + i, 0)),
                  pl.BlockSpec((tm, d), lambda b, i: (k1 + b * nt + i, 0)),
                  pl.BlockSpec((tm, LANES), lambda b, i: (k0 + b * nt + i, 0)),
                  _const_spec((1, d)), _const_spec((1, d))],
        out_specs=grp.x_spec(d),
        out_shape=jax.ShapeDtypeStruct(x.shape, F32),
        compiler_params=_cparams(2),
        name="moe_combine",
    )(x, mod5, y2, y2, gates, lg.reshape(1, d), lb.reshape(1, d))


def _kv_kernel(x_ref, w_ref, k_ref, v_ref, kb_ref, vb_ref):
    bb, tt, d = x_ref.shape
    hd = k_ref.shape[-1]
    xb = x_ref[...].reshape(bb * tt, d).astype(BF16)
    k = _dot(xb, w_ref[:, :hd]).reshape(bb, tt, hd)
    v = _dot(xb, w_ref[:, hd:]).reshape(bb, tt, hd)
    k_ref[...] = k
    v_ref[...] = v
    kb_ref[...] = k.astype(BF16)
    vb_ref[...] = v.astype(BF16)


def _kv_proj(grp, x, w_kv):
    d = x.shape[-1]
    hd = w_kv.shape[1] // 2
    shp = (grp.b, grp.t, hd)
    return pl.pallas_call(
        _kv_kernel,
        grid=grp.grid,
        in_specs=[grp.x_spec(d), _const_spec(w_kv.shape)],
        out_specs=[grp.x_spec(hd)] * 4,
        out_shape=[jax.ShapeDtypeStruct(shp, F32), jax.ShapeDtypeStruct(shp, F32),
                   jax.ShapeDtypeStruct(shp, BF16), jax.ShapeDtypeStruct(shp, BF16)],
        compiler_params=_cparams(2),
        name="kv_proj",
    )(x, w_kv)


def _q_kernel(x_ref, sh_ref, sc_ref, w_ref, q_ref, *, scale):
    bb, tt, d = x_ref.shape
    hb = (x_ref[...] * (1.0 + sc_ref[...]) + sh_ref[...]).reshape(bb * tt, d).astype(BF16)
    q_ref[...] = (_dot(hb, w_ref[...]) * scale).reshape(q_ref.shape).astype(BF16)


def _q_proj(grp, x, mod5, layer, w_q, scale):
    d = x.shape[-1]
    hd = w_q.shape[1]
    return pl.pallas_call(
        functools.partial(_q_kernel, scale=scale),
        grid=grp.grid,
        in_specs=[grp.x_spec(d), grp.mod_spec(d, layer, 0), grp.mod_spec(d, layer, 1), _const_spec(w_q.shape)],
        out_specs=grp.x_spec(hd),
        out_shape=jax.ShapeDtypeStruct((grp.b, grp.t, hd), BF16),
        compiler_params=_cparams(2),
        name="q_proj",
    )(x, mod5, mod5, w_q)


def _o_kernel(o_ref, x_ref, g_ref, w_ref, lg_ref, lb_ref, out_ref, *, alpha):
    bb, tt, d = x_ref.shape
    y = _dot(o_ref[...].reshape(bb * tt, o_ref.shape[-1]).astype(BF16), w_ref[...]).reshape(bb, tt, d)
    v = alpha * x_ref[...] + (1.0 + g_ref[...]) * y
    out_ref[...] = _layer_norm(v, lg_ref[...], lb_ref[...])


def _o_proj(grp, o, x, mod5, layer, w_o, lg, lb, alpha):
    d = x.shape[-1]
    hd = w_o.shape[0]
    return pl.pallas_call(
        functools.partial(_o_kernel, alpha=alpha),
        grid=grp.grid,
        in_specs=[grp.x_spec(hd), grp.x_spec(d), grp.mod_spec(d, layer, 2), _const_spec(w_o.shape),
                  _const_spec((1, d)), _const_spec((1, d))],
        out_specs=grp.x_spec(d),
        out_shape=jax.ShapeDtypeStruct(x.shape, F32),
        compiler_params=_cparams(2),
        name="o_proj",
    )(o, x, mod5, w_o, lg.reshape(1, d), lb.reshape(1, d))


ATTN_Q_BLOCK = 256
ATTN_CACHE_BLOCK = 256
_NT_DIMS = (((1,), (1,)), ((), ()))


def _softplus(z):
    return jnp.maximum(z, 0.0) + jnp.log(1.0 + jnp.exp(-jnp.abs(z)))


def _neg_after(bk):
    jj = lax.broadcasted_iota(jnp.int32, (bk, bk), 0)
    ss = lax.broadcasted_iota(jnp.int32, (bk, bk), 1)
    return jnp.where(jj > ss, -1.0, 0.0).astype(BF16)


def _sb_weights(z, carry, neg_after, valid):
    sp = _softplus(z)
    spm = sp if valid is None else jnp.where(valid, sp, 0.0)
    hi = spm.astype(BF16)
    lo = (spm - hi.astype(F32)).astype(BF16)
    tail = _dot(hi, neg_after) + _dot(lo, neg_after) + carry
    w = jnp.exp(z - sp + tail)
    if valid is not None:
        w = jnp.where(valid, w, 0.0)
    return w, carry - jnp.sum(spm, axis=1, keepdims=True)


def _alive(carries):
    m = carries[0]
    for c in carries[1:]:
        m = jnp.maximum(m, c)
    return jnp.max(m) > TAIL_CUTOFF


def _attn_prompt_kernel(q_ref, kn_ref, vn_ref, o_ref, *, bq, head_dim):
    i = pl.program_id(2)
    w = q_ref.shape[-1]
    nh = w // head_dim
    q = q_ref[...]
    lane = lax.broadcasted_iota(jnp.int32, (bq, w), 1)
    in_head = [(lane >= h * head_dim) & (lane < (h + 1) * head_dim) for h in range(nh)]
    qs = [jnp.where(in_head[h], q, jnp.zeros_like(q)) for h in range(nh)]
    na = _neg_after(bq)
    row = lax.broadcasted_iota(jnp.int32, (bq, bq), 0)
    col = lax.broadcasted_iota(jnp.int32, (bq, bq), 1)
    causal = col < row

    def block(k0, carries, valid):
        kb = kn_ref[pl.ds(k0, bq), :]
        vb = vn_ref[pl.ds(k0, bq), :]
        pvs, cs = [], []
        for h in range(nh):
            z = lax.dot_general(qs[h], kb, _NT_DIMS, preferred_element_type=F32)
            wgt, c = _sb_weights(z, carries[h], na, valid)
            pvs.append(_dot(wgt.astype(BF16), vb))
            cs.append(c)
        return pvs, cs

    zero_c = [jnp.zeros((bq, 1), F32)] * nh
    pv0, c0 = block(pl.multiple_of(i * bq, bq), zero_c, causal)
    has_prev = i > 0
    pv1, c1 = block(pl.multiple_of(jnp.maximum(i - 1, 0) * bq, bq), c0, None)
    acc = jnp.zeros((bq, w), F32)
    carries = []
    for h in range(nh):
        acc = jnp.where(in_head[h], pv0[h] + jnp.where(has_prev, pv1[h], 0.0), acc)
        carries.append(jnp.where(has_prev, c1[h], c0[h]))

    def cond(st):
        return (st[0] >= 0) & _alive(st[1])

    def body(st):
        jb, cs, ac = st
        pvs, cs = block(pl.multiple_of(jb * bq, bq), cs, None)
        for h in range(nh):
            ac = jnp.where(in_head[h], ac + pvs[h], ac)
        return jb - 1, tuple(cs), ac

    _, _, acc = lax.while_loop(cond, body, (i - 2, tuple(carries), acc))
    o_ref[...] = acc.astype(o_ref.dtype)


def _attention_prompt(q, kn, vn, head_dim):
    b, t, hd = q.shape
    w = LANES if hd % LANES == 0 else hd
    bq = _pick_tile(t, ATTN_Q_BLOCK)
    return pl.pallas_call(
        functools.partial(_attn_prompt_kernel, bq=bq, head_dim=head_dim),
        grid=(b, hd // w, t // bq),
        in_specs=[pl.BlockSpec((None, bq, w), lambda b_, h_, i: (b_, i, h_)),
                  pl.BlockSpec((None, t, w), lambda b_, h_, i: (b_, 0, h_)),
                  pl.BlockSpec((None, t, w), lambda b_, h_, i: (b_, 0, h_))],
        out_specs=pl.BlockSpec((None, bq, w), lambda b_, h_, i: (b_, i, h_)),
        out_shape=jax.ShapeDtypeStruct((b, t, hd), BF16),
        compiler_params=_cparams(3),
        name="sb_attention_prompt",
    )(q, kn, vn)


def _attn_sample_kernel(q_ref, kn_ref, vn_ref, ck_hbm, cv_hbm, o_ref, kbuf, vbuf, sem, *, bk):
    b = pl.program_id(0)
    nb = pl.num_programs(0)
    n_heads, t, _ = q_ref.shape
    ncb = ck_hbm.shape[3] // bk

    def copies(bi, blk, slot):
        p0 = pl.multiple_of(blk * bk, bk)
        return (pltpu.make_async_copy(ck_hbm.at[bi, :, :, pl.ds(p0, bk)], kbuf.at[slot], sem.at[0, slot]),
                pltpu.make_async_copy(cv_hbm.at[bi, :, :, pl.ds(p0, bk)], vbuf.at[slot], sem.at[1, slot]))

    def start(bi, blk, slot):
        for c in copies(bi, blk, slot):
            c.start()

    def wait(bi, blk, slot):
        for c in copies(bi, blk, slot):
            c.wait()

    slot = b % 2

    @pl.when(b == 0)
    def _():
        start(0, ncb - 1, 0)

    @pl.when(b + 1 < nb)
    def _():
        start(b + 1, ncb - 1, 1 - slot)

    def head_rows(x, h):
        return x[h * t:(h + 1) * t]

    row = lax.broadcasted_iota(jnp.int32, (n_heads * t, t), 0)
    col = lax.broadcasted_iota(jnp.int32, (n_heads * t, t), 1)
    causal = col < lax.rem(row, t)
    z = jnp.concatenate([lax.dot_general(q_ref[h], kn_ref[h], _NT_DIMS, preferred_element_type=F32)
                         for h in range(n_heads)], axis=0)
    wgt, carry = _sb_weights(z, jnp.zeros((n_heads * t, 1), F32), _neg_after(t), causal)
    wgt = wgt.astype(BF16)
    acc = jnp.concatenate([_dot(head_rows(wgt, h), vn_ref[h]) for h in range(n_heads)], axis=0)

    na = _neg_after(bk)

    def cache_block(s, carry, acc):
        z = jnp.concatenate([_dot(q_ref[h], kbuf[s, h].astype(BF16)) for h in range(n_heads)], axis=0)
        wgt, carry = _sb_weights(z, carry, na, None)
        wgt = wgt.astype(BF16)
        pv = jnp.concatenate(
            [lax.dot_general(head_rows(wgt, h), vbuf[s, h].astype(BF16), _NT_DIMS, preferred_element_type=F32)
             for h in range(n_heads)], axis=0)
        return carry, acc + pv

    wait(b, ncb - 1, slot)
    carry, acc = cache_block(slot, carry, acc)

    def cond(st):
        return (st[0] >= 0) & (jnp.max(st[1]) > TAIL_CUTOFF)

    def body(st):
        jb, c, a = st
        start(b, jb, 2)
        wait(b, jb, 2)
        c, a = cache_block(2, c, a)
        return jb - 1, c, a

    _, _, acc = lax.while_loop(cond, body, (ncb - 2, carry, acc))
    for h in range(n_heads):
        o_ref[h] = head_rows(acc, h)


def _attention_sample(q4, kn4, vn4, cache_kt, cache_vt):
    b, n_heads, t, dh = q4.shape
    bk = _pick_tile(cache_kt.shape[3], ATTN_CACHE_BLOCK)
    head_spec = pl.BlockSpec((None, n_heads, t, dh), lambda b_: (b_, 0, 0, 0))
    return pl.pallas_call(
        functools.partial(_attn_sample_kernel, bk=bk),
        grid=(b,),
        in_specs=[head_spec, head_spec, head_spec,
                  pl.BlockSpec(memory_space=pl.ANY), pl.BlockSpec(memory_space=pl.ANY)],
        out_specs=head_spec,
        out_shape=jax.ShapeDtypeStruct((b, n_heads, t, dh), F32),
        scratch_shapes=[pltpu.VMEM((3, n_heads, dh, bk), F32), pltpu.VMEM((3, n_heads, dh, bk), F32),
                        pltpu.SemaphoreType.DMA((2, 3))],
        compiler_params=_cparams(1),
        name="sb_attention_sample",
    )(q4, kn4, vn4, cache_kt, cache_vt)


ROWS_PER_STEP = 512
CONV_CONTEXT_ROWS = 32


def kernel(x_prompt, x_sample, cache_k, cache_v, state_conv, c_prompt, c_sample, w_ada, b_ada, ln_g, ln_b, conv_w_in, conv_b_in, conv_w_dw, conv_b_dw, conv_ln_g, conv_ln_b, conv_w_out, conv_b_out, w_kv, w_q, w_o, ffn_w1, ffn_w3, ffn_w2, moe_router, moe_router_b, moe_w1, moe_w3, moe_w2):
    depth, d, _ = w_ada.shape
    n_a = conv_w_in.shape[0]
    kw = conv_w_dw.shape[1]
    n_heads, head_dim = cache_k.shape[2], cache_k.shape[3]
    hd = n_heads * head_dim
    n_experts = moe_router.shape[-1]
    alpha = float((2 * depth) ** 0.25)
    bp, tp, _ = x_prompt.shape
    bs, ts, _ = x_sample.shape
    past = cache_k.shape[1]
    hp = CONV_CONTEXT_ROWS
    assert kw - 1 <= hp and n_experts <= LANES

    c_all = jnp.concatenate([c_sample, c_prompt], axis=0)
    mod = _ada_table(c_all, w_ada, b_ada)
    mod5 = mod.reshape(depth, bs + bp, 6, 1, d)

    g_prompt = _Group(bp, tp, bs, ROWS_PER_STEP)
    g_sample = _Group(bs, ts, 0, ROWS_PER_STEP)

    cw_in, cw_out = conv_w_in.astype(BF16), conv_w_out.astype(BF16)
    wkv_b, wq_b, wo_b = w_kv.astype(BF16), w_q.astype(BF16), w_o.astype(BF16)
    f1_b, f3_b, f2_b = ffn_w1.astype(BF16), ffn_w3.astype(BF16), ffn_w2.astype(BF16)
    e1_b, e3_b, e2_b = moe_w1.astype(BF16), moe_w3.astype(BF16), moe_w2.astype(BF16)
    w_dw8 = conv_w_dw.reshape(n_a, kw, 1, d)
    rw_pad = jnp.pad(moe_router, ((0, 0), (0, 0), (0, LANES - n_experts)))
    rb_pad = jnp.pad(moe_router_b, ((0, 0), (0, LANES - n_experts))).reshape(-1, 1, LANES)
    cache_k2 = cache_k.transpose(0, 2, 3, 1)
    cache_v2 = cache_v.transpose(0, 2, 3, 1)

    groups = [dict(grp=g_prompt, x=x_prompt, hist=jnp.zeros((n_a, bp, kw - 1, d), x_prompt.dtype),
                   cache=None, row0=0),
              dict(grp=g_sample, x=x_sample, hist=state_conv, cache=(cache_k2, cache_v2), row0=bp * tp)]
    n_all = bp * tp + bs * ts
    n_pad = -(-n_all // ROWS_PER_STEP) * ROWS_PER_STEP
    for g in groups:
        g["hist_pad"] = jnp.pad(g["hist"], ((0, 0), (0, 0), (hp - (kw - 1), 0), (0, 0)))
        g["new_hist"] = []

    def heads_major(a):
        return a.reshape(a.shape[0], a.shape[1], n_heads, head_dim).transpose(0, 2, 1, 3)

    for l in range(depth):
        m = l // 2
        for g in groups:
            grp, x = g["grp"], g["x"]
            if l < n_a:
                t = grp.t
                u = _conv_in(grp, x, mod5, l, cw_in[l], conv_b_in[l])
                x = _conv_out(grp, u, g["hist_pad"][l], x, mod5, l, w_dw8[l], conv_b_dw[l], conv_ln_g[l],
                              conv_ln_b[l], cw_out[l], conv_b_out[l], ln_g[l, 0], ln_b[l, 0], alpha)
                u_ext = jnp.concatenate([g["hist"][l][:, max(0, kw - 1 - t):], u[:, max(0, t - (kw - 1)):]], axis=1)
                g["new_hist"].append(u_ext[:, -(kw - 1):])
            else:
                j = l - n_a
                q = _q_proj(grp, x, mod5, l, wq_b[j], head_dim ** -0.5)
                if g["cache"] is None:
                    o = _attention_prompt(q, g["kb"], g["vb"], head_dim)
                else:
                    o4 = _attention_sample(heads_major(q), g["kb4"], g["vb4"], *g["cache"])
                    o = o4.transpose(0, 2, 1, 3).reshape(grp.b, grp.t, hd)
                x = _o_proj(grp, o, x, mod5, l, wo_b[j], ln_g[l, 0], ln_b[l, 0], alpha)
            if l % 2 == 0:
                x = _ffn(grp, x, mod5, l, f1_b[m], f3_b[m], f2_b[m], ln_g[l, 1], ln_b[l, 1], alpha)
            g["x"] = x
        if l % 2 == 1:
            routed = [_router(g["grp"], g["x"], mod5, l, rw_pad[m], rb_pad[m], n_experts) for g in groups]
            h_all, idx_all, gates_all = (jnp.concatenate(parts, axis=0) for parts in zip(*routed))
            y2 = _moe_experts(h_all, idx_all[:, :TOP_K], e1_b, e3_b, e2_b, m, n_pad)
            for g in groups:
                g["x"] = _moe_combine(g["grp"], g["x"], mod5, l, y2, gates_all, g["row0"], n_pad,
                                      ln_g[l, 1], ln_b[l, 1], alpha)
        if l == n_a - 1:
            for g in groups:
                g["k"], g["v"], g["kb"], g["vb"] = _kv_proj(g["grp"], g["x"], wkv_b)
                if g["cache"] is not None:
                    g["kb4"], g["vb4"] = heads_major(g["kb"]), heads_major(g["vb"])

    def outputs(g):
        shp = (g["grp"].b, g["grp"].t, n_heads, head_dim)
        return g["x"], g["k"].reshape(shp), g["v"].reshape(shp), jnp.stack(g["new_hist"], axis=0)

    y_p, k_p, v_p, conv_p = outputs(groups[0])
    y_s, k_s, v_s, conv_s = outputs(groups[1])
    return (y_p, y_s, k_p, v_p, conv_p, k_s, v_s, conv_s)
```

```python
import functools

import jax
import jax.numpy as jnp
from jax import lax
from jax.experimental import pallas as pl
from jax.experimental.pallas import tpu as pltpu

LN_EPS = 1e-5
TOP_K = 2
LANES = 128
SUBLANES = 8
VMEM_LIMIT_BYTES = 56 * 1024 * 1024
TAIL_CUTOFF = -110.0

F32 = jnp.float32
BF16 = jnp.bfloat16


def _cparams(n_axes):
    return pltpu.CompilerParams(dimension_semantics=("arbitrary",) * n_axes,
                                vmem_limit_bytes=VMEM_LIMIT_BYTES)


def _dot(a, b):
    return jnp.dot(a, b, preferred_element_type=F32)


def _layer_norm(v, g, b):
    mu = jnp.mean(v, axis=-1, keepdims=True)
    c = v - mu
    var = jnp.mean(c * c, axis=-1, keepdims=True)
    return c * lax.rsqrt(var + LN_EPS) * g + b


def _silu(v):
    return v * jax.nn.sigmoid(v)


def _pick_tile(n, target):
    t = min(n, target)
    while n % t:
        t -= 1
    return t


def _ada_kernel(c_ref, w_ref, b_ref, o_ref):
    ca = _silu(c_ref[...]).astype(BF16)
    o_ref[...] = _dot(ca, w_ref[...].astype(BF16)) + b_ref[...]


def _ada_table(c_all, w_ada, b_ada):
    n_layers, d, d6 = w_ada.shape
    bc = c_all.shape[0]
    tn = _pick_tile(d6, 1536)
    return pl.pallas_call(
        _ada_kernel,
        grid=(n_layers, d6 // tn),
        in_specs=[pl.BlockSpec((bc, d), lambda l, j: (0, 0)),
                  pl.BlockSpec((None, d, tn), lambda l, j: (l, 0, j)),
                  pl.BlockSpec((None, 1, tn), lambda l, j: (l, 0, j))],
        out_specs=pl.BlockSpec((None, bc, tn), lambda l, j: (l, 0, j)),
        out_shape=jax.ShapeDtypeStruct((n_layers, bc, d6), F32),
        compiler_params=_cparams(2),
        name="ada_table",
    )(c_all, w_ada, b_ada.reshape(n_layers, 1, d6))


class _Group:
    def __init__(self, b, t, row_off, rows_target):
        self.b, self.t = b, t
        self.tt = _pick_tile(t, rows_target)
        self.bb = _pick_tile(b, max(1, rows_target // self.tt))
        assert row_off % self.bb == 0
        self.boff = row_off // self.bb
        self.grid = (b // self.bb, t // self.tt)
        self.tm = self.bb * self.tt

    def x_spec(self, d):
        return pl.BlockSpec((self.bb, self.tt, d), lambda b, i: (b, i, 0))

    def mod_spec(self, d, layer, which):
        boff = self.boff
        return pl.BlockSpec((None, self.bb, None, 1, d),
                            lambda b, i: (layer, boff + b, which, 0, 0))


def _const_spec(shape):
    nd = len(shape)
    return pl.BlockSpec(shape, lambda b, i: (0,) * nd)


def _conv_in_kernel(x_ref, sh_ref, sc_ref, w_ref, b_ref, u_ref):
    bb, tt, d = x_ref.shape
    h = x_ref[...] * (1.0 + sc_ref[...]) + sh_ref[...]
    hb = h.reshape(bb * tt, d).astype(BF16)
    a = _dot(hb, w_ref[:, :d]) + b_ref[:, :d]
    g = _dot(hb, w_ref[:, d:]) + b_ref[:, d:]
    u_ref[...] = (a * jax.nn.sigmoid(g)).reshape(bb, tt, d)


def _conv_in(grp, x, mod5, layer, w_in, b_in):
    d = x.shape[-1]
    return pl.pallas_call(
        _conv_in_kernel,
        grid=grp.grid,
        in_specs=[grp.x_spec(d), grp.mod_spec(d, layer, 0), grp.mod_spec(d, layer, 1),
                  _const_spec(w_in.shape), _const_spec((1, 2 * d))],
        out_specs=grp.x_spec(d),
        out_shape=jax.ShapeDtypeStruct(x.shape, F32),
        compiler_params=_cparams(2),
        name="conv_in",
    )(x, mod5, mod5, w_in, b_in.reshape(1, 2 * d))


CONV_CHUNK_ROWS = 64
CONV_CHUNK_LANES = 256


def _conv_out_kernel(*refs, alpha, kw, use_halo):
    if use_halo:
        (u_ref, halo_ref, hist_ref, x_ref, g_ref, wdw_ref, bdw_ref, cg_ref, cb_ref,
         w_ref, b_ref, lg_ref, lb_ref, o_ref, ext_ref, z_ref, win_ref) = refs
    else:
        (u_ref, hist_ref, x_ref, g_ref, wdw_ref, bdw_ref, cg_ref, cb_ref,
         w_ref, b_ref, lg_ref, lb_ref, o_ref, ext_ref, z_ref, win_ref) = refs
    bb, tt, d = u_ref.shape
    hp = ext_ref.shape[1] - tt
    ext_ref[:, hp:, :] = u_ref[...]
    if use_halo:
        first = pl.program_id(1) == 0

        @pl.when(first)
        def _():
            ext_ref[:, :hp, :] = hist_ref[...]

        @pl.when(jnp.logical_not(first))
        def _():
            ext_ref[:, :hp, :] = halo_ref[...]
    else:
        ext_ref[:, :hp, :] = hist_ref[...]

    off = hp - (kw - 1)
    bc, _, cc = win_ref.shape
    rc = min(tt, CONV_CHUNK_ROWS)
    for b0 in range(0, bb, bc):
        for r0 in range(0, tt, rc):
            for c0 in range(0, d, cc):
                acc = jnp.broadcast_to(bdw_ref[:, c0:c0 + cc], (bc, rc, cc))
                for r in range(SUBLANES):
                    taps = [k for k in range(kw) if (off + k) % SUBLANES == r]
                    if not taps:
                        continue
                    span = ((off + taps[-1]) // SUBLANES) * SUBLANES + rc
                    win_ref[:, :span, :] = ext_ref[b0:b0 + bc, r0 + r:r0 + r + span, c0:c0 + cc]
                    for k in taps:
                        a0 = ((off + k) // SUBLANES) * SUBLANES
                        acc = acc + wdw_ref[k, :, c0:c0 + cc] * win_ref[:, a0:a0 + rc, :]
                z_ref[b0:b0 + bc, r0:r0 + rc, c0:c0 + cc] = acc

    z = z_ref[...].reshape(bb * tt, d)
    za = _silu(_layer_norm(z, cg_ref[...], cb_ref[...])).astype(BF16)
    y = (_dot(za, w_ref[...]) + b_ref[...]).reshape(bb, tt, d)
    v = alpha * x_ref[...] + (1.0 + g_ref[...]) * y
    o_ref[...] = _layer_norm(v, lg_ref[...], lb_ref[...])


def _conv_out(grp, u, hist_pad, x, mod5, layer, w_dw8, b_dw, cg, cb, w_out, b_out, lg, lb, alpha):
    d = x.shape[-1]
    kw = w_dw8.shape[0]
    hp = hist_pad.shape[1]
    use_halo = grp.grid[1] > 1
    tt = grp.tt
    rc = min(tt, CONV_CHUNK_ROWS)
    bc = min(grp.bb, CONV_CHUNK_ROWS // rc)
    row = lambda a: a.reshape(1, d)
    in_specs = [grp.x_spec(d)]
    args = [u]
    if use_halo:
        assert grp.bb == 1 and tt % hp == 0
        in_specs.append(pl.BlockSpec((1, hp, d), lambda b, i: (b, jnp.maximum(i * (tt // hp) - 1, 0), 0)))
        args.append(u)
    in_specs += [pl.BlockSpec((grp.bb, hp, d), lambda b, i: (b, 0, 0)),
                 grp.x_spec(d), grp.mod_spec(d, layer, 2),
                 _const_spec(w_dw8.shape), _const_spec((1, d)), _const_spec((1, d)), _const_spec((1, d)),
                 _const_spec(w_out.shape), _const_spec((1, d)), _const_spec((1, d)), _const_spec((1, d))]
    args += [hist_pad, x, mod5, w_dw8, row(b_dw), row(cg), row(cb), w_out, row(b_out), row(lg), row(lb)]
    return pl.pallas_call(
        functools.partial(_conv_out_kernel, alpha=alpha, kw=kw, use_halo=use_halo),
        grid=grp.grid,
        in_specs=in_specs,
        out_specs=grp.x_spec(d),
        out_shape=jax.ShapeDtypeStruct(x.shape, F32),
        scratch_shapes=[pltpu.VMEM((grp.bb, hp + tt, d), F32), pltpu.VMEM((grp.bb, tt, d), F32),
                        pltpu.VMEM((bc, rc + hp, min(d, CONV_CHUNK_LANES)), F32)],
        compiler_params=_cparams(2),
        name="conv_out",
    )(*args)


def _ffn_kernel(x_ref, sh_ref, sc_ref, g_ref, w1_ref, w3_ref, w2_ref, lg_ref, lb_ref, o_ref, *, alpha, fc):
    bb, tt, d = x_ref.shape
    f = w1_ref.shape[1]
    x = x_ref[...]
    hb = (x * (1.0 + sc_ref[...]) + sh_ref[...]).reshape(bb * tt, d).astype(BF16)
    y = jnp.zeros((bb * tt, d), F32)
    for f0 in range(0, f, fc):
        a = _dot(hb, w1_ref[:, f0:f0 + fc])
        b = _dot(hb, w3_ref[:, f0:f0 + fc])
        y = y + _dot((_silu(a) * b).astype(BF16), w2_ref[f0:f0 + fc, :])
    v = alpha * x + (1.0 + g_ref[...]) * y.reshape(bb, tt, d)
    o_ref[...] = _layer_norm(v, lg_ref[...], lb_ref[...])


def _ffn(grp, x, mod5, layer, w1, w3, w2, lg, lb, alpha):
    d = x.shape[-1]
    f = w1.shape[1]
    fc = f // 2 if (f // 2) % LANES == 0 else f
    return pl.pallas_call(
        functools.partial(_ffn_kernel, alpha=alpha, fc=fc),
        grid=grp.grid,
        in_specs=[grp.x_spec(d), grp.mod_spec(d, layer, 3), grp.mod_spec(d, layer, 4), grp.mod_spec(d, layer, 5),
                  _const_spec(w1.shape), _const_spec(w3.shape), _const_spec(w2.shape),
                  _const_spec((1, d)), _const_spec((1, d))],
        out_specs=grp.x_spec(d),
        out_shape=jax.ShapeDtypeStruct(x.shape, F32),
        compiler_params=_cparams(2),
        name="ffn",
    )(x, mod5, mod5, mod5, w1, w3, w2, lg.reshape(1, d), lb.reshape(1, d))


def _router_kernel(x_ref, sh_ref, sc_ref, rw_ref, rb_ref, h_ref, idx_ref, gate_ref, *, n_experts):
    bb, tt, d = x_ref.shape
    h = (x_ref[...] * (1.0 + sc_ref[...]) + sh_ref[...]).reshape(bb * tt, d)
    h_ref[...] = h
    rw = rw_ref[...]
    h_hi = h.astype(BF16)
    h_lo = (h - h_hi.astype(F32)).astype(BF16)
    r_hi = rw.astype(BF16)
    r_lo = (rw - r_hi.astype(F32)).astype(BF16)
    logits = _dot(h_hi, r_hi) + (_dot(h_lo, r_hi) + _dot(h_hi, r_lo)) + rb_ref[...]
    lane = lax.broadcasted_iota(jnp.int32, logits.shape, 1)
    neg = jnp.float32(-jnp.inf)
    lg = jnp.where(lane < n_experts, logits, neg)
    m1 = jnp.max(lg, axis=1, keepdims=True)
    i1 = jnp.min(jnp.where(lg == m1, lane, LANES), axis=1, keepdims=True)
    lg2 = jnp.where(lane == i1, neg, lg)
    m2 = jnp.max(lg2, axis=1, keepdims=True)
    i2 = jnp.min(jnp.where(lg2 == m2, lane, LANES), axis=1, keepdims=True)
    e2 = jnp.exp(m2 - m1)
    den = 1.0 + e2
    idx_ref[...] = jnp.where(lane == 0, i1, jnp.where(lane == 1, i2, 0))
    gate_ref[...] = jnp.where(lane == 0, 1.0 / den, jnp.where(lane == 1, e2 / den, 0.0))


def _router(grp, x, mod5, layer, rw_pad, rb_pad, n_experts):
    d = x.shape[-1]
    n = grp.b * grp.t
    nt = grp.grid[1]
    row_spec = lambda w: pl.BlockSpec((grp.tm, w), lambda b, i: (b * nt + i, 0))
    return pl.pallas_call(
        functools.partial(_router_kernel, n_experts=n_experts),
        grid=grp.grid,
        in_specs=[grp.x_spec(d), grp.mod_spec(d, layer, 3), grp.mod_spec(d, layer, 4),
                  _const_spec(rw_pad.shape), _const_spec((1, LANES))],
        out_specs=[row_spec(d), row_spec(LANES), row_spec(LANES)],
        out_shape=[jax.ShapeDtypeStruct((n, d), F32),
                   jax.ShapeDtypeStruct((n, LANES), jnp.int32),
                   jax.ShapeDtypeStruct((n, LANES), F32)],
        compiler_params=_cparams(2),
        name="router",
    )(x, mod5, mod5, rw_pad, rb_pad)


MOE_ROW_TILE = 896
MOE_F_TILE = 512
DMA_UNROLL = 8
MOE_ISSUE_POINTS = 4


def _moe_kernel(texp_ref, tval_ref, src_cur_ref, src_nxt_ref, dst_cur_ref, dst_prv_ref, h_hbm,
                w1_ref, w3_ref, w2_ref, out_hbm, xbuf, xb16, acc, stage, gsem, ssem, *, n_f_steps, dump_row0):
    i = pl.program_id(0)
    j = pl.program_id(1)
    nt = pl.num_programs(0)
    tm = xb16.shape[0]
    rc = tm // n_f_steps
    slot = i % 2
    other = 1 - slot
    valid = tval_ref[i] == 1
    prev_valid = (i > 0) & (tval_ref[jnp.maximum(i - 1, 0)] == 1)
    next_valid = (i + 1 < nt) & (tval_ref[jnp.minimum(i + 1, nt - 1)] == 1)

    def gather_copy(row, r, s):
        return pltpu.make_async_copy(h_hbm.at[pl.ds(row, 1)], xbuf.at[s, pl.ds(r, 1)], gsem.at[s])

    def scatter_copy(row, r, s):
        return pltpu.make_async_copy(stage.at[s, pl.ds(r, 1)], out_hbm.at[pl.ds(row, 1)], ssem.at[s])

    def loop_start(copy_fn, idx_ref, s):
        def body(r, c):
            copy_fn(idx_ref[0, 0, r], r, s).start()
            return c
        lax.fori_loop(0, tm, body, 0, unroll=DMA_UNROLL)

    def wait_gather(s):
        pltpu.make_async_copy(h_hbm.at[pl.ds(0, tm)], xbuf.at[s], gsem.at[s]).wait()

    def wait_scatter(s):
        pltpu.make_async_copy(stage.at[s], out_hbm.at[pl.ds(0, tm)], ssem.at[s]).wait()

    @pl.when(j == 0)
    def _():
        @pl.when(i == 0)
        def _():
            loop_start(gather_copy, src_cur_ref, 0)
            stage[1] = jnp.zeros(stage.shape[1:], stage.dtype)

        @pl.when(valid | prev_valid)
        def _():
            wait_gather(slot)

        @pl.when(valid)
        def _():
            xb16[...] = xbuf[slot].astype(BF16)
            acc[...] = jnp.zeros_like(acc)

        @pl.when(jnp.logical_not(valid))
        def _():
            acc[...] = jnp.zeros_like(acc)
            row0 = pl.multiple_of(dst_cur_ref[0, 0, 0], SUBLANES)
            fill = pltpu.make_async_copy(acc, out_hbm.at[pl.ds(row0, tm)], ssem.at[slot])
            fill.start()
            fill.wait()

    @pl.when(valid)
    def _():
        base = pl.multiple_of(j * rc, SUBLANES)
        bounds = [(p * rc) // MOE_ISSUE_POINTS for p in range(MOE_ISSUE_POINTS + 1)]

        def issue(part):
            for r in range(bounds[part], bounds[part + 1]):
                gather_copy(src_nxt_ref[0, 0, base + r], base + r, other).start()
            for r in range(bounds[part], bounds[part + 1]):
                row = jnp.where(i > 0, dst_prv_ref[0, 0, base + r], dump_row0 + base + r)
                scatter_copy(row, base + r, other).start()

        xb = xb16[...]
        issue(0)
        a = _dot(xb, w1_ref[...])
        issue(1)
        b = _dot(xb, w3_ref[...])
        issue(2)
        acc[...] += _dot((_silu(a) * b).astype(BF16), w2_ref[...])
        issue(3)

    @pl.when(valid & (j == n_f_steps - 1))
    def _():
        wait_scatter(other)
        stage[slot] = acc[...]

        @pl.when(jnp.logical_not(next_valid))
        def _():
            loop_start(scatter_copy, dst_cur_ref, slot)
            wait_scatter(slot)

            @pl.when(i == nt - 1)
            def _():
                wait_gather(other)


def _moe_plan(idx, n_experts, tm, n_pad):
    n = idx.shape[0]
    m = TOP_K * n
    n_tiles = -(-m // tm) + n_experts
    e_flat = idx.T.reshape(m)
    order = jnp.argsort(e_flat, stable=True).astype(jnp.int32)
    counts = jnp.sum(e_flat[:, None] == jnp.arange(n_experts)[None, :], axis=0).astype(jnp.int32)
    padded = ((counts + tm - 1) // tm) * tm
    pend = jnp.cumsum(padded)
    pstart = pend - padded
    ustart = jnp.cumsum(counts) - counts
    tile_first = jnp.arange(n_tiles, dtype=jnp.int32) * tm
    tile_exp = jnp.minimum(jnp.sum(tile_first[:, None] >= pend[None, :], axis=1), n_experts - 1).astype(jnp.int32)
    tile_valid = (tile_first < pend[-1]).astype(jnp.int32)
    p = jnp.arange(n_tiles * tm, dtype=jnp.int32)
    pe = jnp.repeat(tile_exp, tm)
    within = p - pstart[pe]
    real = (within < counts[pe]) & jnp.repeat(tile_valid == 1, tm)
    slot = order[jnp.clip(ustart[pe] + within, 0, m - 1)]
    src = jnp.where(real, slot % n, 0).astype(jnp.int32)
    pad_rank = jnp.cumsum(jnp.logical_not(real).astype(jnp.int32)) - 1
    dst = jnp.where(real, (slot // n) * n_pad + slot % n, TOP_K * n_pad + pad_rank).astype(jnp.int32)
    return (tile_exp, tile_valid, src.reshape(n_tiles, 1, tm), dst.reshape(n_tiles, 1, tm), n_tiles)


def _moe_experts(h, idx, w1, w3, w2, lm, n_pad):
    n, d = h.shape
    _, n_experts, _, f = w1.shape
    tf = _pick_tile(f, MOE_F_TILE)
    n_f_steps = f // tf
    tm = (MOE_ROW_TILE // n_f_steps) * n_f_steps
    assert tm % SUBLANES == 0 and (TOP_K * n) % SUBLANES == 0 and n_pad % SUBLANES == 0
    assert (tm // n_f_steps) % SUBLANES == 0
    tile_exp, tile_valid, src, dst, n_tiles = _moe_plan(idx, n_experts, tm, n_pad)
    dump_row0 = TOP_K * n_pad + (n_tiles * tm - TOP_K * n)
    out_rows = dump_row0 + tm
    smem_spec = lambda fn: pl.BlockSpec((1, 1, tm), fn, memory_space=pltpu.SMEM)
    grid_spec = pltpu.PrefetchScalarGridSpec(
        num_scalar_prefetch=2,
        grid=(n_tiles, n_f_steps),
        in_specs=[smem_spec(lambda i, j, te, tv: (i, 0, 0)),
                  smem_spec(lambda i, j, te, tv: (jnp.minimum(i + 1, n_tiles - 1), 0, 0)),
                  smem_spec(lambda i, j, te, tv: (i, 0, 0)),
                  smem_spec(lambda i, j, te, tv: (jnp.maximum(i - 1, 0), 0, 0)),
                  pl.BlockSpec(memory_space=pl.ANY),
                  pl.BlockSpec((None, None, d, tf), lambda i, j, te, tv: (lm, te[i], 0, j)),
                  pl.BlockSpec((None, None, d, tf), lambda i, j, te, tv: (lm, te[i], 0, j)),
                  pl.BlockSpec((None, None, tf, d), lambda i, j, te, tv: (lm, te[i], j, 0))],
        out_specs=pl.BlockSpec(memory_space=pl.ANY),
        scratch_shapes=[pltpu.VMEM((2, tm, d), F32), pltpu.VMEM((tm, d), BF16), pltpu.VMEM((tm, d), F32),
                        pltpu.VMEM((2, tm, d), F32), pltpu.SemaphoreType.DMA((2,)), pltpu.SemaphoreType.DMA((2,))],
    )
    return pl.pallas_call(
        functools.partial(_moe_kernel, n_f_steps=n_f_steps, dump_row0=dump_row0),
        grid_spec=grid_spec,
        out_shape=jax.ShapeDtypeStruct((out_rows, d), F32),
        compiler_params=_cparams(2),
        name="moe_experts",
    )(tile_exp, tile_valid, src, src, dst, dst, h, w1, w3, w2)


def _combine_kernel(x_ref, g_ref, y0_ref, y1_ref, gate_ref, lg_ref, lb_ref, o_ref, *, alpha):
    bb, tt, d = x_ref.shape
    gt = gate_ref[...]
    y = gt[:, 0:1] * y0_ref[...] + gt[:, 1:2] * y1_ref[...]
    v = alpha * x_ref[...] + (1.0 + g_ref[...]) * y.reshape(bb, tt, d)
    o_ref[...] = _layer_norm(v, lg_ref[...], lb_ref[...])


def _moe_combine(grp, x, mod5, layer, y2, gates, row0, n_pad, lg, lb, alpha):
    d = x.shape[-1]
    nt = grp.grid[1]
    tm = grp.tm
    assert row0 % tm == 0 and n_pad % tm == 0
    k0 = row0 // tm
    k1 = (n_pad + row0) // tm
    return pl.pallas_call(
        functools.partial(_combine_kernel, alpha=alpha),
        grid=grp.grid,
        in_specs=[grp.x_spec(d), grp.mod_spec(d, layer, 5),
                  pl.BlockSpec((tm, d), lambda b, i: (k0 + b * nt + i, 0)),
                  pl.BlockSpec((tm, d), lambda b, i: (k1 + b * nt + i, 0)),
                  pl.BlockSpec((tm, LANES), lambda b, i: (k0 + b * nt + i, 0)),
                  _const_spec((1, d)), _const_spec((1, d))],
        out_specs=grp.x_spec(d),
        out_shape=jax.ShapeDtypeStruct(x.shape, F32),
        compiler_params=_cparams(2),
        name="moe_combine",
    )(x, mod5, y2, y2, gates, lg.reshape(1, d), lb.reshape(1, d))


def _kv_kernel(x_ref, w_ref, k_ref, v_ref, kb_ref, vb_ref):
    bb, tt, d = x_ref.shape
    hd = k_ref.shape[-1]
    xb = x_ref[...].reshape(bb * tt, d).astype(BF16)
    k = _dot(xb, w_ref[:, :hd]).reshape(bb, tt, hd)
    v = _dot(xb, w_ref[:, hd:]).reshape(bb, tt, hd)
    k_ref[...] = k
    v_ref[...] = v
    kb_ref[...] = k.astype(BF16)
    vb_ref[...] = v.astype(BF16)


def _kv_proj(grp, x, w_kv):
    d = x.shape[-1]
    hd = w_kv.shape[1] // 2
    shp = (grp.b, grp.t, hd)
    return pl.pallas_call(
        _kv_kernel,
        grid=grp.grid,
        in_specs=[grp.x_spec(d), _const_spec(w_kv.shape)],
        out_specs=[grp.x_spec(hd)] * 4,
        out_shape=[jax.ShapeDtypeStruct(shp, F32), jax.ShapeDtypeStruct(shp, F32),
                   jax.ShapeDtypeStruct(shp, BF16), jax.ShapeDtypeStruct(shp, BF16)],
        compiler_params=_cparams(2),
        name="kv_proj",
    )(x, w_kv)


def _q_kernel(x_ref, sh_ref, sc_ref, w_ref, q_ref, *, scale):
    bb, tt, d = x_ref.shape
    hb = (x_ref[...] * (1.0 + sc_ref[...]) + sh_ref[...]).reshape(bb * tt, d).astype(BF16)
    q_ref[...] = (_dot(hb, w_ref[...]) * scale).reshape(q_ref.shape).astype(BF16)


def _q_proj(grp, x, mod5, layer, w_q, scale):
    d = x.shape[-1]
    hd = w_q.shape[1]
    return pl.pallas_call(
        functools.partial(_q_kernel, scale=scale),
        grid=grp.grid,
        in_specs=[grp.x_spec(d), grp.mod_spec(d, layer, 0), grp.mod_spec(d, layer, 1), _const_spec(w_q.shape)],
        out_specs=grp.x_spec(hd),
        out_shape=jax.ShapeDtypeStruct((grp.b, grp.t, hd), BF16),
        compiler_params=_cparams(2),
        name="q_proj",
    )(x, mod5, mod5, w_q)


def _o_kernel(o_ref, x_ref, g_ref, w_ref, lg_ref, lb_ref, out_ref, *, alpha):
    bb, tt, d = x_ref.shape
    y = _dot(o_ref[...].reshape(bb * tt, o_ref.shape[-1]).astype(BF16), w_ref[...]).reshape(bb, tt, d)
    v = alpha * x_ref[...] + (1.0 + g_ref[...]) * y
    out_ref[...] = _layer_norm(v, lg_ref[...], lb_ref[...])


def _o_proj(grp, o, x, mod5, layer, w_o, lg, lb, alpha):
    d = x.shape[-1]
    hd = w_o.shape[0]
    return pl.pallas_call(
        functools.partial(_o_kernel, alpha=alpha),
        grid=grp.grid,
        in_specs=[grp.x_spec(hd), grp.x_spec(d), grp.mod_spec(d, layer, 2), _const_spec(w_o.shape),
                  _const_spec((1, d)), _const_spec((1, d))],
        out_specs=grp.x_spec(d),
        out_shape=jax.ShapeDtypeStruct(x.shape, F32),
        compiler_params=_cparams(2),
        name="o_proj",
    )(o, x, mod5, w_o, lg.reshape(1, d), lb.reshape(1, d))


ATTN_Q_BLOCK = 256
ATTN_CACHE_BLOCK = 256
ATTN_HEAD_LANES = 256
_NT_DIMS = (((1,), (1,)), ((), ()))


def _softplus(z):
    return jnp.maximum(z, 0.0) + jnp.log(1.0 + jnp.exp(-jnp.abs(z)))


def _neg_after(bk):
    jj = lax.broadcasted_iota(jnp.int32, (bk, bk), 0)
    ss = lax.broadcasted_iota(jnp.int32, (bk, bk), 1)
    return jnp.where(jj > ss, -1.0, 0.0).astype(BF16)


def _sb_weights(z, carry, neg_after, valid):
    sp = _softplus(z)
    spm = sp if valid is None else jnp.where(valid, sp, 0.0)
    hi = spm.astype(BF16)
    lo = (spm - hi.astype(F32)).astype(BF16)
    tail = _dot(hi, neg_after) + _dot(lo, neg_after) + carry
    w = jnp.exp(z - sp + tail)
    if valid is not None:
        w = jnp.where(valid, w, 0.0)
    return w, carry - jnp.sum(spm, axis=1, keepdims=True)


def _alive(carries):
    m = carries[0]
    for c in carries[1:]:
        m = jnp.maximum(m, c)
    return jnp.max(m) > TAIL_CUTOFF


def _attn_prompt_kernel(q_ref, kn_ref, vn_ref, o_ref, *, bq, head_dim):
    i = pl.program_id(2)
    w = q_ref.shape[-1]
    nh = w // head_dim
    q = q_ref[...]
    lane = lax.broadcasted_iota(jnp.int32, (bq, w), 1)
    in_head = [(lane >= h * head_dim) & (lane < (h + 1) * head_dim) for h in range(nh)]
    qs = [jnp.where(in_head[h], q, jnp.zeros_like(q)) for h in range(nh)]
    na = _neg_after(bq)
    row = lax.broadcasted_iota(jnp.int32, (bq, bq), 0)
    col = lax.broadcasted_iota(jnp.int32, (bq, bq), 1)
    causal = col < row

    def scores(k0):
        kb = kn_ref[pl.ds(k0, bq), :]
        return [lax.dot_general(qs[h], kb, _NT_DIMS, preferred_element_type=F32) for h in range(nh)]

    def keep_sums(zs, valid):
        sps = [_softplus(z) for z in zs]
        spms = sps if valid is None else [jnp.where(valid, sp, 0.0) for sp in sps]
        his = [s.astype(BF16) for s in spms]
        los = [(s - hi.astype(F32)).astype(BF16) for s, hi in zip(spms, his)]
        tails = [_dot(hi, na) + _dot(lo, na) for hi, lo in zip(his, los)]
        sums = [jnp.sum(s, axis=1, keepdims=True) for s in spms]
        return sps, tails, sums

    def weighted_values(k0, zs, sps, tails, carries, valid):
        vb = vn_ref[pl.ds(k0, bq), :]
        wgts = [jnp.exp(z - sp + (tl + c)) for z, sp, tl, c in zip(zs, sps, tails, carries)]
        if valid is not None:
            wgts = [jnp.where(valid, wg, 0.0) for wg in wgts]
        return [_dot(wg.astype(BF16), vb) for wg in wgts]

    d0 = pl.multiple_of(i * bq, bq)
    p0 = pl.multiple_of(jnp.maximum(i - 1, 0) * bq, bq)
    has_prev = i > 0
    z_d, z_p = scores(d0), scores(p0)
    sp_d, tail_d, sum_d = keep_sums(z_d, causal)
    sp_p, tail_p, sum_p = keep_sums(z_p, None)
    c_d = [-s for s in sum_d]
    pv_d = weighted_values(d0, z_d, sp_d, tail_d, [jnp.zeros((bq, 1), F32)] * nh, causal)
    pv_p = weighted_values(p0, z_p, sp_p, tail_p, c_d, None)
    acc = jnp.zeros((bq, w), F32)
    carries = []
    for h in range(nh):
        acc = jnp.where(in_head[h], pv_d[h] + jnp.where(has_prev, pv_p[h], 0.0), acc)
        carries.append(jnp.where(has_prev, c_d[h] - sum_p[h], c_d[h]))

    def cond(st):
        return (st[0] >= 0) & _alive(st[1])

    def body(st):
        jb, cs, ac = st
        k0 = pl.multiple_of(jb * bq, bq)
        zs = scores(k0)
        sps, tails, sums = keep_sums(zs, None)
        pvs = weighted_values(k0, zs, sps, tails, cs, None)
        for h in range(nh):
            ac = jnp.where(in_head[h], ac + pvs[h], ac)
        return jb - 1, tuple(c - s for c, s in zip(cs, sums)), ac

    _, _, acc = lax.while_loop(cond, body, (i - 2, tuple(carries), acc))
    o_ref[...] = acc.astype(o_ref.dtype)


def _attention_prompt(q, kn, vn, head_dim):
    b, t, hd = q.shape
    w = ATTN_HEAD_LANES if hd % ATTN_HEAD_LANES == 0 else hd
    bq = _pick_tile(t, ATTN_Q_BLOCK)
    return pl.pallas_call(
        functools.partial(_attn_prompt_kernel, bq=bq, head_dim=head_dim),
        grid=(b, hd // w, t // bq),
        in_specs=[pl.BlockSpec((None, bq, w), lambda b_, h_, i: (b_, i, h_)),
                  pl.BlockSpec((None, t, w), lambda b_, h_, i: (b_, 0, h_)),
                  pl.BlockSpec((None, t, w), lambda b_, h_, i: (b_, 0, h_))],
        out_specs=pl.BlockSpec((None, bq, w), lambda b_, h_, i: (b_, i, h_)),
        out_shape=jax.ShapeDtypeStruct((b, t, hd), BF16),
        compiler_params=_cparams(3),
        name="sb_attention_prompt",
    )(q, kn, vn)


def _attn_sample_kernel(q_ref, kn_ref, vn_ref, ck_hbm, cv_hbm, o_ref, kbuf, vbuf, sem, *, bk):
    b = pl.program_id(0)
    nb = pl.num_programs(0)
    n_heads, t, _ = q_ref.shape
    ncb = ck_hbm.shape[3] // bk

    def copies(bi, blk, slot):
        p0 = pl.multiple_of(blk * bk, bk)
        return (pltpu.make_async_copy(ck_hbm.at[bi, :, :, pl.ds(p0, bk)], kbuf.at[slot], sem.at[0, slot]),
                pltpu.make_async_copy(cv_hbm.at[bi, :, :, pl.ds(p0, bk)], vbuf.at[slot], sem.at[1, slot]))

    def start(bi, blk, slot):
        for c in copies(bi, blk, slot):
            c.start()

    def wait(bi, blk, slot):
        for c in copies(bi, blk, slot):
            c.wait()

    slot = b % 2

    @pl.when(b == 0)
    def _():
        start(0, ncb - 1, 0)

    @pl.when(b + 1 < nb)
    def _():
        start(b + 1, ncb - 1, 1 - slot)

    def head_rows(x, h):
        return x[h * t:(h + 1) * t]

    row = lax.broadcasted_iota(jnp.int32, (n_heads * t, t), 0)
    col = lax.broadcasted_iota(jnp.int32, (n_heads * t, t), 1)
    causal = col < lax.rem(row, t)
    z = jnp.concatenate([lax.dot_general(q_ref[h], kn_ref[h], _NT_DIMS, preferred_element_type=F32)
                         for h in range(n_heads)], axis=0)
    wgt, carry = _sb_weights(z, jnp.zeros((n_heads * t, 1), F32), _neg_after(t), causal)
    wgt = wgt.astype(BF16)
    acc = jnp.concatenate([_dot(head_rows(wgt, h), vn_ref[h]) for h in range(n_heads)], axis=0)

    na = _neg_after(bk)

    def cache_block(s, carry, acc):
        z = jnp.concatenate([_dot(q_ref[h], kbuf[s, h].astype(BF16)) for h in range(n_heads)], axis=0)
        wgt, carry = _sb_weights(z, carry, na, None)
        wgt = wgt.astype(BF16)
        pv = jnp.concatenate(
            [lax.dot_general(head_rows(wgt, h), vbuf[s, h].astype(BF16), _NT_DIMS, preferred_element_type=F32)
             for h in range(n_heads)], axis=0)
        return carry, acc + pv

    wait(b, ncb - 1, slot)
    carry, acc = cache_block(slot, carry, acc)

    def cond(st):
        return (st[0] >= 0) & (jnp.max(st[1]) > TAIL_CUTOFF)

    def body(st):
        jb, c, a = st
        start(b, jb, 2)
        wait(b, jb, 2)
        c, a = cache_block(2, c, a)
        return jb - 1, c, a

    _, _, acc = lax.while_loop(cond, body, (ncb - 2, carry, acc))
    for h in range(n_heads):
        o_ref[h] = head_rows(acc, h)


def _attention_sample(q4, kn4, vn4, cache_kt, cache_vt):
    b, n_heads, t, dh = q4.shape
    bk = _pick_tile(cache_kt.shape[3], ATTN_CACHE_BLOCK)
    head_spec = pl.BlockSpec((None, n_heads, t, dh), lambda b_: (b_, 0, 0, 0))
    return pl.pallas_call(
        functools.partial(_attn_sample_kernel, bk=bk),
        grid=(b,),
        in_specs=[head_spec, head_spec, head_spec,
                  pl.BlockSpec(memory_space=pl.ANY), pl.BlockSpec(memory_space=pl.ANY)],
        out_specs=head_spec,
        out_shape=jax.ShapeDtypeStruct((b, n_heads, t, dh), F32),
        scratch_shapes=[pltpu.VMEM((3, n_heads, dh, bk), F32), pltpu.VMEM((3, n_heads, dh, bk), F32),
                        pltpu.SemaphoreType.DMA((2, 3))],
        compiler_params=_cparams(1),
        name="sb_attention_sample",
    )(q4, kn4, vn4, cache_kt, cache_vt)


ROWS_PER_STEP = 512
CONV_CONTEXT_ROWS = 32


def kernel(x_prompt, x_sample, cache_k, cache_v, state_conv, c_prompt, c_sample, w_ada, b_ada, ln_g, ln_b, conv_w_in, conv_b_in, conv_w_dw, conv_b_dw, conv_ln_g, conv_ln_b, conv_w_out, conv_b_out, w_kv, w_q, w_o, ffn_w1, ffn_w3, ffn_w2, moe_router, moe_router_b, moe_w1, moe_w3, moe_w2):
    depth, d, _ = w_ada.shape
    n_a = conv_w_in.shape[0]
    kw = conv_w_dw.shape[1]
    n_heads, head_dim = cache_k.shape[2], cache_k.shape[3]
    hd = n_heads * head_dim
    n_experts = moe_router.shape[-1]
    alpha = float((2 * depth) ** 0.25)
    bp, tp, _ = x_prompt.shape
    bs, ts, _ = x_sample.shape
    past = cache_k.shape[1]
    hp = CONV_CONTEXT_ROWS
    assert kw - 1 <= hp and n_experts <= LANES

    c_all = jnp.concatenate([c_sample, c_prompt], axis=0)
    mod = _ada_table(c_all, w_ada, b_ada)
    mod5 = mod.reshape(depth, bs + bp, 6, 1, d)

    g_prompt = _Group(bp, tp, bs, ROWS_PER_STEP)
    g_sample = _Group(bs, ts, 0, ROWS_PER_STEP)

    cw_in, cw_out = conv_w_in.astype(BF16), conv_w_out.astype(BF16)
    wkv_b, wq_b, wo_b = w_kv.astype(BF16), w_q.astype(BF16), w_o.astype(BF16)
    f1_b, f3_b, f2_b = ffn_w1.astype(BF16), ffn_w3.astype(BF16), ffn_w2.astype(BF16)
    e1_b, e3_b, e2_b = moe_w1.astype(BF16), moe_w3.astype(BF16), moe_w2.astype(BF16)
    w_dw8 = conv_w_dw.reshape(n_a, kw, 1, d)
    rw_pad = jnp.pad(moe_router, ((0, 0), (0, 0), (0, LANES - n_experts)))
    rb_pad = jnp.pad(moe_router_b, ((0, 0), (0, LANES - n_experts))).reshape(-1, 1, LANES)
    cache_k2 = cache_k.transpose(0, 2, 3, 1)
    cache_v2 = cache_v.transpose(0, 2, 3, 1)

    groups = [dict(grp=g_prompt, x=x_prompt, hist=jnp.zeros((n_a, bp, kw - 1, d), x_prompt.dtype),
                   cache=None, row0=0),
              dict(grp=g_sample, x=x_sample, hist=state_conv, cache=(cache_k2, cache_v2), row0=bp * tp)]
    n_all = bp * tp + bs * ts
    n_pad = -(-n_all // ROWS_PER_STEP) * ROWS_PER_STEP
    for g in groups:
        g["hist_pad"] = jnp.pad(g["hist"], ((0, 0), (0, 0), (hp - (kw - 1), 0), (0, 0)))
        g["new_hist"] = []

    def heads_major(a):
        return a.reshape(a.shape[0], a.shape[1], n_heads, head_dim).transpose(0, 2, 1, 3)

    for l in range(depth):
        m = l // 2
        for g in groups:
            grp, x = g["grp"], g["x"]
            if l < n_a:
                t = grp.t
                u = _conv_in(grp, x, mod5, l, cw_in[l], conv_b_in[l])
                x = _conv_out(grp, u, g["hist_pad"][l], x, mod5, l, w_dw8[l], conv_b_dw[l], conv_ln_g[l],
                              conv_ln_b[l], cw_out[l], conv_b_out[l], ln_g[l, 0], ln_b[l, 0], alpha)
                u_ext = jnp.concatenate([g["hist"][l][:, max(0, kw - 1 - t):], u[:, max(0, t - (kw - 1)):]], axis=1)
                g["new_hist"].append(u_ext[:, -(kw - 1):])
            else:
                j = l - n_a
                q = _q_proj(grp, x, mod5, l, wq_b[j], head_dim ** -0.5)
                if g["cache"] is None:
                    o = _attention_prompt(q, g["kb"], g["vb"], head_dim)
                else:
                    o4 = _attention_sample(heads_major(q), g["kb4"], g["vb4"], *g["cache"])
                    o = o4.transpose(0, 2, 1, 3).reshape(grp.b, grp.t, hd)
                x = _o_proj(grp, o, x, mod5, l, wo_b[j], ln_g[l, 0], ln_b[l, 0], alpha)
            if l % 2 == 0:
                x = _ffn(grp, x, mod5, l, f1_b[m], f3_b[m], f2_b[m], ln_g[l, 1], ln_b[l, 1], alpha)
            g["x"] = x
        if l % 2 == 1:
            routed = [_router(g["grp"], g["x"], mod5, l, rw_pad[m], rb_pad[m], n_experts) for g in groups]
            h_all, idx_all, gates_all = (jnp.concatenate(parts, axis=0) for parts in zip(*routed))
            y2 = _moe_experts(h_all, idx_all[:, :TOP_K], e1_b, e3_b, e2_b, m, n_pad)
            for g in groups:
                g["x"] = _moe_combine(g["grp"], g["x"], mod5, l, y2, gates_all, g["row0"], n_pad,
                                      ln_g[l, 1], ln_b[l, 1], alpha)
        if l == n_a - 1:
            for g in groups:
                g["k"], g["v"], g["kb"], g["vb"] = _kv_proj(g["grp"], g["x"], wkv_b)
                if g["cache"] is not None:
                    g["kb4"], g["vb4"] = heads_major(g["kb"]), heads_major(g["vb"])

    def outputs(g):
        shp = (g["grp"].b, g["grp"].t, n_heads, head_dim)
        return g["x"], g["k"].reshape(shp), g["v"].reshape(shp), jnp.stack(g["new_hist"], axis=0)

    y_p, k_p, v_p, conv_p = outputs(groups[0])
    y_s, k_s, v_s, conv_s = outputs(groups[1])
    return (y_p, y_s, k_p, v_p, conv_p, k_s, v_s, conv_s)
```

```python
import functools

import jax
import jax.numpy as jnp
from jax import lax
from jax.experimental import pallas as pl
from jax.experimental.pallas import tpu as pltpu

LN_EPS = 1e-5
TOP_K = 2
LANES = 128
SUBLANES = 8
VMEM_LIMIT_BYTES = 56 * 1024 * 1024
TAIL_CUTOFF = -110.0

F32 = jnp.float32
BF16 = jnp.bfloat16


def _cparams(n_axes):
    return pltpu.CompilerParams(dimension_semantics=("arbitrary",) * n_axes,
                                vmem_limit_bytes=VMEM_LIMIT_BYTES)


def _dot(a, b):
    return jnp.dot(a, b, preferred_element_type=F32)


def _layer_norm(v, g, b):
    mu = jnp.mean(v, axis=-1, keepdims=True)
    c = v - mu
    var = jnp.mean(c * c, axis=-1, keepdims=True)
    return c * lax.rsqrt(var + LN_EPS) * g + b


def _silu(v):
    return v * jax.nn.sigmoid(v)


def _pick_tile(n, target):
    t = min(n, target)
    while n % t:
        t -= 1
    return t


def _ada_kernel(c_ref, w_ref, b_ref, o_ref):
    ca = _silu(c_ref[...]).astype(BF16)
    o_ref[...] = _dot(ca, w_ref[...].astype(BF16)) + b_ref[...]


def _ada_table(c_all, w_ada, b_ada):
    n_layers, d, d6 = w_ada.shape
    bc = c_all.shape[0]
    tn = _pick_tile(d6, 1536)
    return pl.pallas_call(
        _ada_kernel,
        grid=(n_layers, d6 // tn),
        in_specs=[pl.BlockSpec((bc, d), lambda l, j: (0, 0)),
                  pl.BlockSpec((None, d, tn), lambda l, j: (l, 0, j)),
                  pl.BlockSpec((None, 1, tn), lambda l, j: (l, 0, j))],
        out_specs=pl.BlockSpec((None, bc, tn), lambda l, j: (l, 0, j)),
        out_shape=jax.ShapeDtypeStruct((n_layers, bc, d6), F32),
        compiler_params=_cparams(2),
        name="ada_table",
    )(c_all, w_ada, b_ada.reshape(n_layers, 1, d6))


class _Group:
    def __init__(self, b, t, row_off, rows_target):
        self.b, self.t = b, t
        self.tt = _pick_tile(t, rows_target)
        self.bb = _pick_tile(b, max(1, rows_target // self.tt))
        assert row_off % self.bb == 0
        self.boff = row_off // self.bb
        self.grid = (b // self.bb, t // self.tt)
        self.tm = self.bb * self.tt

    def x_spec(self, d):
        return pl.BlockSpec((self.bb, self.tt, d), lambda b, i: (b, i, 0))

    def mod_spec(self, d, layer, which):
        boff = self.boff
        return pl.BlockSpec((None, self.bb, None, 1, d),
                            lambda b, i: (layer, boff + b, which, 0, 0))


def _const_spec(shape):
    nd = len(shape)
    return pl.BlockSpec(shape, lambda b, i: (0,) * nd)


def _conv_in_kernel(x_ref, sh_ref, sc_ref, w_ref, b_ref, u_ref):
    bb, tt, d = x_ref.shape
    h = x_ref[...] * (1.0 + sc_ref[...]) + sh_ref[...]
    hb = h.reshape(bb * tt, d).astype(BF16)
    a = _dot(hb, w_ref[:, :d]) + b_ref[:, :d]
    g = _dot(hb, w_ref[:, d:]) + b_ref[:, d:]
    u_ref[...] = (a * jax.nn.sigmoid(g)).reshape(bb, tt, d)


def _conv_in(grp, x, mod5, layer, w_in, b_in):
    d = x.shape[-1]
    return pl.pallas_call(
        _conv_in_kernel,
        grid=grp.grid,
        in_specs=[grp.x_spec(d), grp.mod_spec(d, layer, 0), grp.mod_spec(d, layer, 1),
                  _const_spec(w_in.shape), _const_spec((1, 2 * d))],
        out_specs=grp.x_spec(d),
        out_shape=jax.ShapeDtypeStruct(x.shape, F32),
        compiler_params=_cparams(2),
        name="conv_in",
    )(x, mod5, mod5, w_in, b_in.reshape(1, 2 * d))


CONV_CHUNK_ROWS = 64
CONV_CHUNK_LANES = 256


def _conv_out_kernel(*refs, alpha, kw, use_halo):
    if use_halo:
        (u_ref, halo_ref, hist_ref, x_ref, g_ref, wdw_ref, bdw_ref, cg_ref, cb_ref,
         w_ref, b_ref, lg_ref, lb_ref, o_ref, ext_ref, z_ref, win_ref) = refs
    else:
        (u_ref, hist_ref, x_ref, g_ref, wdw_ref, bdw_ref, cg_ref, cb_ref,
         w_ref, b_ref, lg_ref, lb_ref, o_ref, ext_ref, z_ref, win_ref) = refs
    bb, tt, d = u_ref.shape
    hp = ext_ref.shape[1] - tt
    ext_ref[:, hp:, :] = u_ref[...]
    if use_halo:
        first = pl.program_id(1) == 0

        @pl.when(first)
        def _():
            ext_ref[:, :hp, :] = hist_ref[...]

        @pl.when(jnp.logical_not(first))
        def _():
            ext_ref[:, :hp, :] = halo_ref[...]
    else:
        ext_ref[:, :hp, :] = hist_ref[...]

    off = hp - (kw - 1)
    bc, _, cc = win_ref.shape
    rc = min(tt, CONV_CHUNK_ROWS)
    for b0 in range(0, bb, bc):
        for r0 in range(0, tt, rc):
            for c0 in range(0, d, cc):
                acc = jnp.broadcast_to(bdw_ref[:, c0:c0 + cc], (bc, rc, cc))
                for r in range(SUBLANES):
                    taps = [k for k in range(kw) if (off + k) % SUBLANES == r]
                    if not taps:
                        continue
                    span = ((off + taps[-1]) // SUBLANES) * SUBLANES + rc
                    win_ref[:, :span, :] = ext_ref[b0:b0 + bc, r0 + r:r0 + r + span, c0:c0 + cc]
                    for k in taps:
                        a0 = ((off + k) // SUBLANES) * SUBLANES
                        acc = acc + wdw_ref[k, :, c0:c0 + cc] * win_ref[:, a0:a0 + rc, :]
                z_ref[b0:b0 + bc, r0:r0 + rc, c0:c0 + cc] = acc

    z = z_ref[...].reshape(bb * tt, d)
    za = _silu(_layer_norm(z, cg_ref[...], cb_ref[...])).astype(BF16)
    y = (_dot(za, w_ref[...]) + b_ref[...]).reshape(bb, tt, d)
    v = alpha * x_ref[...] + (1.0 + g_ref[...]) * y
    o_ref[...] = _layer_norm(v, lg_ref[...], lb_ref[...])


def _conv_out(grp, u, hist_pad, x, mod5, layer, w_dw8, b_dw, cg, cb, w_out, b_out, lg, lb, alpha):
    d = x.shape[-1]
    kw = w_dw8.shape[0]
    hp = hist_pad.shape[1]
    use_halo = grp.grid[1] > 1
    tt = grp.tt
    rc = min(tt, CONV_CHUNK_ROWS)
    bc = min(grp.bb, CONV_CHUNK_ROWS // rc)
    row = lambda a: a.reshape(1, d)
    in_specs = [grp.x_spec(d)]
    args = [u]
    if use_halo:
        assert grp.bb == 1 and tt % hp == 0
        in_specs.append(pl.BlockSpec((1, hp, d), lambda b, i: (b, jnp.maximum(i * (tt // hp) - 1, 0), 0)))
        args.append(u)
    in_specs += [pl.BlockSpec((grp.bb, hp, d), lambda b, i: (b, 0, 0)),
                 grp.x_spec(d), grp.mod_spec(d, layer, 2),
                 _const_spec(w_dw8.shape), _const_spec((1, d)), _const_spec((1, d)), _const_spec((1, d)),
                 _const_spec(w_out.shape), _const_spec((1, d)), _const_spec((1, d)), _const_spec((1, d))]
    args += [hist_pad, x, mod5, w_dw8, row(b_dw), row(cg), row(cb), w_out, row(b_out), row(lg), row(lb)]
    return pl.pallas_call(
        functools.partial(_conv_out_kernel, alpha=alpha, kw=kw, use_halo=use_halo),
        grid=grp.grid,
        in_specs=in_specs,
        out_specs=grp.x_spec(d),
        out_shape=jax.ShapeDtypeStruct(x.shape, F32),
        scratch_shapes=[pltpu.VMEM((grp.bb, hp + tt, d), F32), pltpu.VMEM((grp.bb, tt, d), F32),
                        pltpu.VMEM((bc, rc + hp, min(d, CONV_CHUNK_LANES)), F32)],
        compiler_params=_cparams(2),
        name="conv_out",
    )(*args)


def _ffn_kernel(x_ref, sh_ref, sc_ref, g_ref, w1_ref, w3_ref, w2_ref, lg_ref, lb_ref, o_ref, *, alpha, fc):
    bb, tt, d = x_ref.shape
    f = w1_ref.shape[1]
    x = x_ref[...]
    hb = (x * (1.0 + sc_ref[...]) + sh_ref[...]).reshape(bb * tt, d).astype(BF16)
    y = jnp.zeros((bb * tt, d), F32)
    for f0 in range(0, f, fc):
        a = _dot(hb, w1_ref[:, f0:f0 + fc])
        b = _dot(hb, w3_ref[:, f0:f0 + fc])
        y = y + _dot((_silu(a) * b).astype(BF16), w2_ref[f0:f0 + fc, :])
    v = alpha * x + (1.0 + g_ref[...]) * y.reshape(bb, tt, d)
    o_ref[...] = _layer_norm(v, lg_ref[...], lb_ref[...])


def _ffn(grp, x, mod5, layer, w1, w3, w2, lg, lb, alpha):
    d = x.shape[-1]
    f = w1.shape[1]
    fc = f // 2 if (f // 2) % LANES == 0 else f
    return pl.pallas_call(
        functools.partial(_ffn_kernel, alpha=alpha, fc=fc),
        grid=grp.grid,
        in_specs=[grp.x_spec(d), grp.mod_spec(d, layer, 3), grp.mod_spec(d, layer, 4), grp.mod_spec(d, layer, 5),
                  _const_spec(w1.shape), _const_spec(w3.shape), _const_spec(w2.shape),
                  _const_spec((1, d)), _const_spec((1, d))],
        out_specs=grp.x_spec(d),
        out_shape=jax.ShapeDtypeStruct(x.shape, F32),
        compiler_params=_cparams(2),
        name="ffn",
    )(x, mod5, mod5, mod5, w1, w3, w2, lg.reshape(1, d), lb.reshape(1, d))


def _router_kernel(x_ref, sh_ref, sc_ref, rw_ref, rb_ref, h_ref, idx_ref, gate_ref, *, n_experts):
    bb, tt, d = x_ref.shape
    h = (x_ref[...] * (1.0 + sc_ref[...]) + sh_ref[...]).reshape(bb * tt, d)
    h_ref[...] = h
    rw = rw_ref[...]
    h_hi = h.astype(BF16)
    h_lo = (h - h_hi.astype(F32)).astype(BF16)
    r_hi = rw.astype(BF16)
    r_lo = (rw - r_hi.astype(F32)).astype(BF16)
    logits = _dot(h_hi, r_hi) + (_dot(h_lo, r_hi) + _dot(h_hi, r_lo)) + rb_ref[...]
    lane = lax.broadcasted_iota(jnp.int32, logits.shape, 1)
    neg = jnp.float32(-jnp.inf)
    lg = jnp.where(lane < n_experts, logits, neg)
    m1 = jnp.max(lg, axis=1, keepdims=True)
    i1 = jnp.min(jnp.where(lg == m1, lane, LANES), axis=1, keepdims=True)
    lg2 = jnp.where(lane == i1, neg, lg)
    m2 = jnp.max(lg2, axis=1, keepdims=True)
    i2 = jnp.min(jnp.where(lg2 == m2, lane, LANES), axis=1, keepdims=True)
    e2 = jnp.exp(m2 - m1)
    den = 1.0 + e2
    idx_ref[...] = jnp.where(lane == 0, i1, jnp.where(lane == 1, i2, 0))
    gate_ref[...] = jnp.where(lane == 0, 1.0 / den, jnp.where(lane == 1, e2 / den, 0.0))


def _router(grp, x, mod5, layer, rw_pad, rb_pad, n_experts):
    d = x.shape[-1]
    n = grp.b * grp.t
    nt = grp.grid[1]
    row_spec = lambda w: pl.BlockSpec((grp.tm, w), lambda b, i: (b * nt + i, 0))
    return pl.pallas_call(
        functools.partial(_router_kernel, n_experts=n_experts),
        grid=grp.grid,
        in_specs=[grp.x_spec(d), grp.mod_spec(d, layer, 3), grp.mod_spec(d, layer, 4),
                  _const_spec(rw_pad.shape), _const_spec((1, LANES))],
        out_specs=[row_spec(d), row_spec(LANES), row_spec(LANES)],
        out_shape=[jax.ShapeDtypeStruct((n, d), F32),
                   jax.ShapeDtypeStruct((n, LANES), jnp.int32),
                   jax.ShapeDtypeStruct((n, LANES), F32)],
        compiler_params=_cparams(2),
        name="router",
    )(x, mod5, mod5, rw_pad, rb_pad)


MOE_ROW_TILE = 896
MOE_F_TILE = 512
DMA_UNROLL = 8
MOE_ISSUE_POINTS = 4


def _moe_kernel(texp_ref, tval_ref, src_cur_ref, src_nxt_ref, dst_cur_ref, dst_prv_ref, h_hbm,
                w1_ref, w3_ref, w2_ref, out_hbm, xbuf, xb16, acc, stage, gsem, ssem, *, n_f_steps, dump_row0):
    i = pl.program_id(0)
    j = pl.program_id(1)
    nt = pl.num_programs(0)
    tm = xb16.shape[0]
    rc = tm // n_f_steps
    slot = i % 2
    other = 1 - slot
    valid = tval_ref[i] == 1
    prev_valid = (i > 0) & (tval_ref[jnp.maximum(i - 1, 0)] == 1)
    next_valid = (i + 1 < nt) & (tval_ref[jnp.minimum(i + 1, nt - 1)] == 1)

    def gather_copy(row, r, s):
        return pltpu.make_async_copy(h_hbm.at[pl.ds(row, 1)], xbuf.at[s, pl.ds(r, 1)], gsem.at[s])

    def scatter_copy(row, r, s):
        return pltpu.make_async_copy(stage.at[s, pl.ds(r, 1)], out_hbm.at[pl.ds(row, 1)], ssem.at[s])

    def loop_start(copy_fn, idx_ref, s):
        def body(r, c):
            copy_fn(idx_ref[0, 0, r], r, s).start()
            return c
        lax.fori_loop(0, tm, body, 0, unroll=DMA_UNROLL)

    def wait_gather(s):
        pltpu.make_async_copy(h_hbm.at[pl.ds(0, tm)], xbuf.at[s], gsem.at[s]).wait()

    def wait_scatter(s):
        pltpu.make_async_copy(stage.at[s], out_hbm.at[pl.ds(0, tm)], ssem.at[s]).wait()

    @pl.when(j == 0)
    def _():
        @pl.when(i == 0)
        def _():
            loop_start(gather_copy, src_cur_ref, 0)
            stage[1] = jnp.zeros(stage.shape[1:], stage.dtype)

        @pl.when(valid | prev_valid)
        def _():
            wait_gather(slot)

        @pl.when(valid)
        def _():
            xb16[...] = xbuf[slot].astype(BF16)
            acc[...] = jnp.zeros_like(acc)

        @pl.when(jnp.logical_not(valid))
        def _():
            acc[...] = jnp.zeros_like(acc)
            row0 = pl.multiple_of(dst_cur_ref[0, 0, 0], SUBLANES)
            fill = pltpu.make_async_copy(acc, out_hbm.at[pl.ds(row0, tm)], ssem.at[slot])
            fill.start()
            fill.wait()

    @pl.when(valid)
    def _():
        base = pl.multiple_of(j * rc, SUBLANES)
        bounds = [(p * rc) // MOE_ISSUE_POINTS for p in range(MOE_ISSUE_POINTS + 1)]

        def issue(part):
            for r in range(bounds[part], bounds[part + 1]):
                gather_copy(src_nxt_ref[0, 0, base + r], base + r, other).start(priority=r % 2)
            for r in range(bounds[part], bounds[part + 1]):
                row = jnp.where(i > 0, dst_prv_ref[0, 0, base + r], dump_row0 + base + r)
                scatter_copy(row, base + r, other).start(priority=r % 2)

        xb = xb16[...]
        issue(0)
        a = _dot(xb, w1_ref[...])
        issue(1)
        b = _dot(xb, w3_ref[...])
        issue(2)
        acc[...] += _dot((_silu(a) * b).astype(BF16), w2_ref[...])
        issue(3)

    @pl.when(valid & (j == n_f_steps - 1))
    def _():
        wait_scatter(other)
        stage[slot] = acc[...]

        @pl.when(jnp.logical_not(next_valid))
        def _():
            loop_start(scatter_copy, dst_cur_ref, slot)
            wait_scatter(slot)

            @pl.when(i == nt - 1)
            def _():
                wait_gather(other)


def _moe_plan(idx, n_experts, tm, n_pad):
    n = idx.shape[0]
    m = TOP_K * n
    n_tiles = -(-m // tm) + n_experts
    e_flat = idx.T.reshape(m)
    order = jnp.argsort(e_flat, stable=True).astype(jnp.int32)
    counts = jnp.sum(e_flat[:, None] == jnp.arange(n_experts)[None, :], axis=0).astype(jnp.int32)
    padded = ((counts + tm - 1) // tm) * tm
    pend = jnp.cumsum(padded)
    pstart = pend - padded
    ustart = jnp.cumsum(counts) - counts
    tile_first = jnp.arange(n_tiles, dtype=jnp.int32) * tm
    tile_exp = jnp.minimum(jnp.sum(tile_first[:, None] >= pend[None, :], axis=1), n_experts - 1).astype(jnp.int32)
    tile_valid = (tile_first < pend[-1]).astype(jnp.int32)
    p = jnp.arange(n_tiles * tm, dtype=jnp.int32)
    pe = jnp.repeat(tile_exp, tm)
    within = p - pstart[pe]
    real = (within < counts[pe]) & jnp.repeat(tile_valid == 1, tm)
    slot = order[jnp.clip(ustart[pe] + within, 0, m - 1)]
    src = jnp.where(real, slot % n, 0).astype(jnp.int32)
    pad_rank = jnp.cumsum(jnp.logical_not(real).astype(jnp.int32)) - 1
    dst = jnp.where(real, (slot // n) * n_pad + slot % n, TOP_K * n_pad + pad_rank).astype(jnp.int32)
    return (tile_exp, tile_valid, src.reshape(n_tiles, 1, tm), dst.reshape(n_tiles, 1, tm), n_tiles)


def _moe_experts(h, idx, w1, w3, w2, lm, n_pad):
    n, d = h.shape
    _, n_experts, _, f = w1.shape
    tf = _pick_tile(f, MOE_F_TILE)
    n_f_steps = f // tf
    tm = (MOE_ROW_TILE // n_f_steps) * n_f_steps
    assert tm % SUBLANES == 0 and (TOP_K * n) % SUBLANES == 0 and n_pad % SUBLANES == 0
    assert (tm // n_f_steps) % SUBLANES == 0
    tile_exp, tile_valid, src, dst, n_tiles = _moe_plan(idx, n_experts, tm, n_pad)
    dump_row0 = TOP_K * n_pad + (n_tiles * tm - TOP_K * n)
    out_rows = dump_row0 + tm
    smem_spec = lambda fn: pl.BlockSpec((1, 1, tm), fn, memory_space=pltpu.SMEM)
    grid_spec = pltpu.PrefetchScalarGridSpec(
        num_scalar_prefetch=2,
        grid=(n_tiles, n_f_steps),
        in_specs=[smem_spec(lambda i, j, te, tv: (i, 0, 0)),
                  smem_spec(lambda i, j, te, tv: (jnp.minimum(i + 1, n_tiles - 1), 0, 0)),
                  smem_spec(lambda i, j, te, tv: (i, 0, 0)),
                  smem_spec(lambda i, j, te, tv: (jnp.maximum(i - 1, 0), 0, 0)),
                  pl.BlockSpec(memory_space=pl.ANY),
                  pl.BlockSpec((None, None, d, tf), lambda i, j, te, tv: (lm, te[i], 0, j)),
                  pl.BlockSpec((None, None, d, tf), lambda i, j, te, tv: (lm, te[i], 0, j)),
                  pl.BlockSpec((None, None, tf, d), lambda i, j, te, tv: (lm, te[i], j, 0))],
        out_specs=pl.BlockSpec(memory_space=pl.ANY),
        scratch_shapes=[pltpu.VMEM((2, tm, d), F32), pltpu.VMEM((tm, d), BF16), pltpu.VMEM((tm, d), F32),
                        pltpu.VMEM((2, tm, d), F32), pltpu.SemaphoreType.DMA((2,)), pltpu.SemaphoreType.DMA((2,))],
    )
    return pl.pallas_call(
        functools.partial(_moe_kernel, n_f_steps=n_f_steps, dump_row0=dump_row0),
        grid_spec=grid_spec,
        out_shape=jax.ShapeDtypeStruct((out_rows, d), F32),
        compiler_params=_cparams(2),
        name="moe_experts",
    )(tile_exp, tile_valid, src, src, dst, dst, h, w1, w3, w2)


def _combine_kernel(x_ref, g_ref, y0_ref, y1_ref, gate_ref, lg_ref, lb_ref, o_ref, *, alpha):
    bb, tt, d = x_ref.shape
    gt = gate_ref[...]
    y = gt[:, 0:1] * y0_ref[...] + gt[:, 1:2] * y1_ref[...]
    v = alpha * x_ref[...] + (1.0 + g_ref[...]) * y.reshape(bb, tt, d)
    o_ref[...] = _layer_norm(v, lg_ref[...], lb_ref[...])


def _moe_combine(grp, x, mod5, layer, y2, gates, row0, n_pad, lg, lb, alpha):
    d = x.shape[-1]
    nt = grp.grid[1]
    tm = grp.tm
    assert row0 % tm == 0 and n_pad % tm == 0
    k0 = row0 // tm
    k1 = (n_pad + row0) // tm
    return pl.pallas_call(
        functools.partial(_combine_kernel, alpha=alpha),
        grid=grp.grid,
        in_specs=[grp.x_spec(d), grp.mod_spec(d, layer, 5),
                  pl.BlockSpec((tm, d), lambda b, i: (k0 + b * nt + i, 0)),
                  pl.BlockSpec((tm, d), lambda b, i: (k1 + b * nt + i, 0)),
                  pl.BlockSpec((tm, LANES), lambda b, i: (k0 + b * nt + i, 0)),
                  _const_spec((1, d)), _const_spec((1, d))],
        out_specs=grp.x_spec(d),
        out_shape=jax.ShapeDtypeStruct(x.shape, F32),
        compiler_params=_cparams(2),
        name="moe_combine",
    )(x, mod5, y2, y2, gates, lg.reshape(1, d), lb.reshape(1, d))


def _kv_kernel(x_ref, w_ref, k_ref, v_ref, kb_ref, vb_ref):
    bb, tt, d = x_ref.shape
    hd = k_ref.shape[-1]
    xb = x_ref[...].reshape(bb * tt, d).astype(BF16)
    k = _dot(xb, w_ref[:, :hd]).reshape(bb, tt, hd)
    v = _dot(xb, w_ref[:, hd:]).reshape(bb, tt, hd)
    k_ref[...] = k
    v_ref[...] = v
    kb_ref[...] = k.astype(BF16)
    vb_ref[...] = v.astype(BF16)


def _kv_proj(grp, x, w_kv):
    d = x.shape[-1]
    hd = w_kv.shape[1] // 2
    shp = (grp.b, grp.t, hd)
    return pl.pallas_call(
        _kv_kernel,
        grid=grp.grid,
        in_specs=[grp.x_spec(d), _const_spec(w_kv.shape)],
        out_specs=[grp.x_spec(hd)] * 4,
        out_shape=[jax.ShapeDtypeStruct(shp, F32), jax.ShapeDtypeStruct(shp, F32),
                   jax.ShapeDtypeStruct(shp, BF16), jax.ShapeDtypeStruct(shp, BF16)],
        compiler_params=_cparams(2),
        name="kv_proj",
    )(x, w_kv)


def _q_kernel(x_ref, sh_ref, sc_ref, w_ref, q_ref, *, scale):
    bb, tt, d = x_ref.shape
    hb = (x_ref[...] * (1.0 + sc_ref[...]) + sh_ref[...]).reshape(bb * tt, d).astype(BF16)
    q_ref[...] = (_dot(hb, w_ref[...]) * scale).reshape(q_ref.shape).astype(BF16)


def _q_proj(grp, x, mod5, layer, w_q, scale):
    d = x.shape[-1]
    hd = w_q.shape[1]
    return pl.pallas_call(
        functools.partial(_q_kernel, scale=scale),
        grid=grp.grid,
        in_specs=[grp.x_spec(d), grp.mod_spec(d, layer, 0), grp.mod_spec(d, layer, 1), _const_spec(w_q.shape)],
        out_specs=grp.x_spec(hd),
        out_shape=jax.ShapeDtypeStruct((grp.b, grp.t, hd), BF16),
        compiler_params=_cparams(2),
        name="q_proj",
    )(x, mod5, mod5, w_q)


def _o_kernel(o_ref, x_ref, g_ref, w_ref, lg_ref, lb_ref, out_ref, *, alpha):
    bb, tt, d = x_ref.shape
    y = _dot(o_ref[...].reshape(bb * tt, o_ref.shape[-1]).astype(BF16), w_ref[...]).reshape(bb, tt, d)
    v = alpha * x_ref[...] + (1.0 + g_ref[...]) * y
    out_ref[...] = _layer_norm(v, lg_ref[...], lb_ref[...])


def _o_proj(grp, o, x, mod5, layer, w_o, lg, lb, alpha):
    d = x.shape[-1]
    hd = w_o.shape[0]
    return pl.pallas_call(
        functools.partial(_o_kernel, alpha=alpha),
        grid=grp.grid,
        in_specs=[grp.x_spec(hd), grp.x_spec(d), grp.mod_spec(d, layer, 2), _const_spec(w_o.shape),
                  _const_spec((1, d)), _const_spec((1, d))],
        out_specs=grp.x_spec(d),
        out_shape=jax.ShapeDtypeStruct(x.shape, F32),
        compiler_params=_cparams(2),
        name="o_proj",
    )(o, x, mod5, w_o, lg.reshape(1, d), lb.reshape(1, d))


ATTN_Q_BLOCK = 256
ATTN_CACHE_BLOCK = 256
ATTN_HEAD_LANES = 256
_NT_DIMS = (((1,), (1,)), ((), ()))


def _softplus(z):
    return jnp.maximum(z, 0.0) + jnp.log(1.0 + jnp.exp(-jnp.abs(z)))


def _neg_after(bk):
    jj = lax.broadcasted_iota(jnp.int32, (bk, bk), 0)
    ss = lax.broadcasted_iota(jnp.int32, (bk, bk), 1)
    return jnp.where(jj > ss, -1.0, 0.0).astype(BF16)


def _sb_weights(z, carry, neg_after, valid):
    sp = _softplus(z)
    spm = sp if valid is None else jnp.where(valid, sp, 0.0)
    hi = spm.astype(BF16)
    lo = (spm - hi.astype(F32)).astype(BF16)
    tail = _dot(hi, neg_after) + _dot(lo, neg_after) + carry
    w = jnp.exp(z - sp + tail)
    if valid is not None:
        w = jnp.where(valid, w, 0.0)
    return w, carry - jnp.sum(spm, axis=1, keepdims=True)


def _alive(carries):
    m = carries[0]
    for c in carries[1:]:
        m = jnp.maximum(m, c)
    return jnp.max(m) > TAIL_CUTOFF


def _attn_prompt_kernel(q_ref, kn_ref, vn_ref, o_ref, *, bq, head_dim):
    i = pl.program_id(2)
    w = q_ref.shape[-1]
    nh = w // head_dim
    q = q_ref[...]
    lane = lax.broadcasted_iota(jnp.int32, (bq, w), 1)
    in_head = [(lane >= h * head_dim) & (lane < (h + 1) * head_dim) for h in range(nh)]
    qs = [jnp.where(in_head[h], q, jnp.zeros_like(q)) for h in range(nh)]
    na = _neg_after(bq)
    row = lax.broadcasted_iota(jnp.int32, (bq, bq), 0)
    col = lax.broadcasted_iota(jnp.int32, (bq, bq), 1)
    causal = col < row

    def scores(k0):
        kb = kn_ref[pl.ds(k0, bq), :]
        return [lax.dot_general(qs[h], kb, _NT_DIMS, preferred_element_type=F32) for h in range(nh)]

    def keep_sums(zs, valid):
        sps = [_softplus(z) for z in zs]
        spms = sps if valid is None else [jnp.where(valid, sp, 0.0) for sp in sps]
        his = [s.astype(BF16) for s in spms]
        los = [(s - hi.astype(F32)).astype(BF16) for s, hi in zip(spms, his)]
        tails = [_dot(hi, na) + _dot(lo, na) for hi, lo in zip(his, los)]
        sums = [jnp.sum(s, axis=1, keepdims=True) for s in spms]
        return sps, tails, sums

    def weighted_values(k0, zs, sps, tails, carries, valid):
        vb = vn_ref[pl.ds(k0, bq), :]
        wgts = [jnp.exp(z - sp + (tl + c)) for z, sp, tl, c in zip(zs, sps, tails, carries)]
        if valid is not None:
            wgts = [jnp.where(valid, wg, 0.0) for wg in wgts]
        return [_dot(wg.astype(BF16), vb) for wg in wgts]

    d0 = pl.multiple_of(i * bq, bq)
    p0 = pl.multiple_of(jnp.maximum(i - 1, 0) * bq, bq)
    has_prev = i > 0
    z_d, z_p = scores(d0), scores(p0)
    sp_d, tail_d, sum_d = keep_sums(z_d, causal)
    sp_p, tail_p, sum_p = keep_sums(z_p, None)
    c_d = [-s for s in sum_d]
    pv_d = weighted_values(d0, z_d, sp_d, tail_d, [jnp.zeros((bq, 1), F32)] * nh, causal)
    pv_p = weighted_values(p0, z_p, sp_p, tail_p, c_d, None)
    acc = jnp.zeros((bq, w), F32)
    carries = []
    for h in range(nh):
        acc = jnp.where(in_head[h], pv_d[h] + jnp.where(has_prev, pv_p[h], 0.0), acc)
        carries.append(jnp.where(has_prev, c_d[h] - sum_p[h], c_d[h]))

    def cond(st):
        return (st[0] >= 0) & _alive(st[1])

    def body(st):
        jb, cs, ac = st
        k0 = pl.multiple_of(jb * bq, bq)
        zs = scores(k0)
        sps, tails, sums = keep_sums(zs, None)
        pvs = weighted_values(k0, zs, sps, tails, cs, None)
        for h in range(nh):
            ac = jnp.where(in_head[h], ac + pvs[h], ac)
        return jb - 1, tuple(c - s for c, s in zip(cs, sums)), ac

    _, _, acc = lax.while_loop(cond, body, (i - 2, tuple(carries), acc))
    o_ref[...] = acc.astype(o_ref.dtype)


def _attention_prompt(q, kn, vn, head_dim):
    b, t, hd = q.shape
    w = ATTN_HEAD_LANES if hd % ATTN_HEAD_LANES == 0 else hd
    bq = _pick_tile(t, ATTN_Q_BLOCK)
    return pl.pallas_call(
        functools.partial(_attn_prompt_kernel, bq=bq, head_dim=head_dim),
        grid=(b, hd // w, t // bq),
        in_specs=[pl.BlockSpec((None, bq, w), lambda b_, h_, i: (b_, i, h_)),
                  pl.BlockSpec((None, t, w), lambda b_, h_, i: (b_, 0, h_)),
                  pl.BlockSpec((None, t, w), lambda b_, h_, i: (b_, 0, h_))],
        out_specs=pl.BlockSpec((None, bq, w), lambda b_, h_, i: (b_, i, h_)),
        out_shape=jax.ShapeDtypeStruct((b, t, hd), BF16),
        compiler_params=_cparams(3),
        name="sb_attention_prompt",
    )(q, kn, vn)


def _attn_sample_kernel(q_ref, kn_ref, vn_ref, ck_hbm, cv_hbm, o_ref, kbuf, vbuf, sem, *, bk):
    b = pl.program_id(0)
    nb = pl.num_programs(0)
    n_heads, t, _ = q_ref.shape
    ncb = ck_hbm.shape[3] // bk

    def copies(bi, blk, slot):
        p0 = pl.multiple_of(blk * bk, bk)
        return (pltpu.make_async_copy(ck_hbm.at[bi, :, :, pl.ds(p0, bk)], kbuf.at[slot], sem.at[0, slot]),
                pltpu.make_async_copy(cv_hbm.at[bi, :, :, pl.ds(p0, bk)], vbuf.at[slot], sem.at[1, slot]))

    def start(bi, blk, slot):
        for c in copies(bi, blk, slot):
            c.start()

    def wait(bi, blk, slot):
        for c in copies(bi, blk, slot):
            c.wait()

    slot = b % 2

    @pl.when(b == 0)
    def _():
        start(0, ncb - 1, 0)

    @pl.when(b + 1 < nb)
    def _():
        start(b + 1, ncb - 1, 1 - slot)

    def head_rows(x, h):
        return x[h * t:(h + 1) * t]

    row = lax.broadcasted_iota(jnp.int32, (n_heads * t, t), 0)
    col = lax.broadcasted_iota(jnp.int32, (n_heads * t, t), 1)
    causal = col < lax.rem(row, t)
    z = jnp.concatenate([lax.dot_general(q_ref[h], kn_ref[h], _NT_DIMS, preferred_element_type=F32)
                         for h in range(n_heads)], axis=0)
    wgt, carry = _sb_weights(z, jnp.zeros((n_heads * t, 1), F32), _neg_after(t), causal)
    wgt = wgt.astype(BF16)
    acc = jnp.concatenate([_dot(head_rows(wgt, h), vn_ref[h]) for h in range(n_heads)], axis=0)

    na = _neg_after(bk)

    def cache_block(s, carry, acc):
        z = jnp.concatenate([_dot(q_ref[h], kbuf[s, h].astype(BF16)) for h in range(n_heads)], axis=0)
        wgt, carry = _sb_weights(z, carry, na, None)
        wgt = wgt.astype(BF16)
        pv = jnp.concatenate(
            [lax.dot_general(head_rows(wgt, h), vbuf[s, h].astype(BF16), _NT_DIMS, preferred_element_type=F32)
             for h in range(n_heads)], axis=0)
        return carry, acc + pv

    wait(b, ncb - 1, slot)
    carry, acc = cache_block(slot, carry, acc)

    def cond(st):
        return (st[0] >= 0) & (jnp.max(st[1]) > TAIL_CUTOFF)

    def body(st):
        jb, c, a = st
        start(b, jb, 2)
        wait(b, jb, 2)
        c, a = cache_block(2, c, a)
        return jb - 1, c, a

    _, _, acc = lax.while_loop(cond, body, (ncb - 2, carry, acc))
    for h in range(n_heads):
        o_ref[h] = head_rows(acc, h)


def _attention_sample(q4, kn4, vn4, cache_kt, cache_vt):
    b, n_heads, t, dh = q4.shape
    bk = _pick_tile(cache_kt.shape[3], ATTN_CACHE_BLOCK)
    head_spec = pl.BlockSpec((None, n_heads, t, dh), lambda b_: (b_, 0, 0, 0))
    return pl.pallas_call(
        functools.partial(_attn_sample_kernel, bk=bk),
        grid=(b,),
        in_specs=[head_spec, head_spec, head_spec,
                  pl.BlockSpec(memory_space=pl.ANY), pl.BlockSpec(memory_space=pl.ANY)],
        out_specs=head_spec,
        out_shape=jax.ShapeDtypeStruct((b, n_heads, t, dh), F32),
        scratch_shapes=[pltpu.VMEM((3, n_heads, dh, bk), F32), pltpu.VMEM((3, n_heads, dh, bk), F32),
                        pltpu.SemaphoreType.DMA((2, 3))],
        compiler_params=_cparams(1),
        name="sb_attention_sample",
    )(q4, kn4, vn4, cache_kt, cache_vt)


ROWS_PER_STEP = 512
CONV_CONTEXT_ROWS = 32


def kernel(x_prompt, x_sample, cache_k, cache_v, state_conv, c_prompt, c_sample, w_ada, b_ada, ln_g, ln_b, conv_w_in, conv_b_in, conv_w_dw, conv_b_dw, conv_ln_g, conv_ln_b, conv_w_out, conv_b_out, w_kv, w_q, w_o, ffn_w1, ffn_w3, ffn_w2, moe_router, moe_router_b, moe_w1, moe_w3, moe_w2):
    depth, d, _ = w_ada.shape
    n_a = conv_w_in.shape[0]
    kw = conv_w_dw.shape[1]
    n_heads, head_dim = cache_k.shape[2], cache_k.shape[3]
    hd = n_heads * head_dim
    n_experts = moe_router.shape[-1]
    alpha = float((2 * depth) ** 0.25)
    bp, tp, _ = x_prompt.shape
    bs, ts, _ = x_sample.shape
    past = cache_k.shape[1]
    hp = CONV_CONTEXT_ROWS
    assert kw - 1 <= hp and n_experts <= LANES

    c_all = jnp.concatenate([c_sample, c_prompt], axis=0)
    mod = _ada_table(c_all, w_ada, b_ada)
    mod5 = mod.reshape(depth, bs + bp, 6, 1, d)

    g_prompt = _Group(bp, tp, bs, ROWS_PER_STEP)
    g_sample = _Group(bs, ts, 0, ROWS_PER_STEP)

    cw_in, cw_out = conv_w_in.astype(BF16), conv_w_out.astype(BF16)
    wkv_b, wq_b, wo_b = w_kv.astype(BF16), w_q.astype(BF16), w_o.astype(BF16)
    f1_b, f3_b, f2_b = ffn_w1.astype(BF16), ffn_w3.astype(BF16), ffn_w2.astype(BF16)
    e1_b, e3_b, e2_b = moe_w1.astype(BF16), moe_w3.astype(BF16), moe_w2.astype(BF16)
    w_dw8 = conv_w_dw.reshape(n_a, kw, 1, d)
    rw_pad = jnp.pad(moe_router, ((0, 0), (0, 0), (0, LANES - n_experts)))
    rb_pad = jnp.pad(moe_router_b, ((0, 0), (0, LANES - n_experts))).reshape(-1, 1, LANES)
    cache_k2 = cache_k.transpose(0, 2, 3, 1)
    cache_v2 = cache_v.transpose(0, 2, 3, 1)

    groups = [dict(grp=g_prompt, x=x_prompt, hist=jnp.zeros((n_a, bp, kw - 1, d), x_prompt.dtype),
                   cache=None, row0=0),
              dict(grp=g_sample, x=x_sample, hist=state_conv, cache=(cache_k2, cache_v2), row0=bp * tp)]
    n_all = bp * tp + bs * ts
    n_pad = -(-n_all // ROWS_PER_STEP) * ROWS_PER_STEP
    for g in groups:
        g["hist_pad"] = jnp.pad(g["hist"], ((0, 0), (0, 0), (hp - (kw - 1), 0), (0, 0)))
        g["new_hist"] = []

    def heads_major(a):
        return a.reshape(a.shape[0], a.shape[1], n_heads, head_dim).transpose(0, 2, 1, 3)

    for l in range(depth):
        m = l // 2
        for g in groups:
            grp, x = g["grp"], g["x"]
            if l < n_a:
                t = grp.t
                u = _conv_in(grp, x, mod5, l, cw_in[l], conv_b_in[l])
                x = _conv_out(grp, u, g["hist_pad"][l], x, mod5, l, w_dw8[l], conv_b_dw[l], conv_ln_g[l],
                              conv_ln_b[l], cw_out[l], conv_b_out[l], ln_g[l, 0], ln_b[l, 0], alpha)
                u_ext = jnp.concatenate([g["hist"][l][:, max(0, kw - 1 - t):], u[:, max(0, t - (kw - 1)):]], axis=1)
                g["new_hist"].append(u_ext[:, -(kw - 1):])
            else:
                j = l - n_a
                q = _q_proj(grp, x, mod5, l, wq_b[j], head_dim ** -0.5)
                if g["cache"] is None:
                    o = _attention_prompt(q, g["kb"], g["vb"], head_dim)
                else:
                    o4 = _attention_sample(heads_major(q), g["kb4"], g["vb4"], *g["cache"])
                    o = o4.transpose(0, 2, 1, 3).reshape(grp.b, grp.t, hd)
                x = _o_proj(grp, o, x, mod5, l, wo_b[j], ln_g[l, 0], ln_b[l, 0], alpha)
            if l % 2 == 0:
                x = _ffn(grp, x, mod5, l, f1_b[m], f3_b[m], f2_b[m], ln_g[l, 1], ln_b[l, 1], alpha)
            g["x"] = x
        if l % 2 == 1:
            routed = [_router(g["grp"], g["x"], mod5, l, rw_pad[m], rb_pad[m], n_experts) for g in groups]
            h_all, idx_all, gates_all = (jnp.concatenate(parts, axis=0) for parts in zip(*routed))
            y2 = _moe_experts(h_all, idx_all[:, :TOP_K], e1_b, e3_b, e2_b, m, n_pad)
            for g in groups:
                g["x"] = _moe_combine(g["grp"], g["x"], mod5, l, y2, gates_all, g["row0"], n_pad,
                                      ln_g[l, 1], ln_b[l, 1], alpha)
        if l == n_a - 1:
            for g in groups:
                g["k"], g["v"], g["kb"], g["vb"] = _kv_proj(g["grp"], g["x"], wkv_b)
                if g["cache"] is not None:
                    g["kb4"], g["vb4"] = heads_major(g["kb"]), heads_major(g["vb"])

    def outputs(g):
        shp = (g["grp"].b, g["grp"].t, n_heads, head_dim)
        return g["x"], g["k"].reshape(shp), g["v"].reshape(shp), jnp.stack(g["new_hist"], axis=0)

    y_p, k_p, v_p, conv_p = outputs(groups[0])
    y_s, k_s, v_s, conv_s = outputs(groups[1])
    return (y_p, y_s, k_p, v_p, conv_p, k_s, v_s, conv_s)
```
